```python
import jax
import jax.numpy as jnp
from jax import lax
import numpy as np

D_MODEL = 1024
BATCH = 8
SEQ = 4096
DEPTH = 1
DEC_BATCH = 32
DEC_SEQ = 16
PAST_LEN = 4096

CHUNK = 64
Q_BLOCK = 128
MIX_WIDTH = D_MODEL
SB_HEADS = 8
SB_HEAD_DIM = MIX_WIDTH // 2 // SB_HEADS
SB_WIDTH = SB_HEADS * SB_HEAD_DIM
MLA_HEADS = 8
MLA_NOPE_DIM = 64
MLA_ROPE_DIM = 32
MLA_V_DIM = (MIX_WIDTH - SB_WIDTH) // MLA_HEADS
MLA_WIDTH = MLA_HEADS * MLA_V_DIM
MLA_Q_RANK = 384
MLA_KV_RANK = 256
MLA_QK_DIM = MLA_NOPE_DIM + MLA_ROPE_DIM
MLA_SCALE = MLA_QK_DIM ** -0.5
ROPE_BASE = 10000.0
IN_SPLITS = (SB_WIDTH, 2 * SB_WIDTH, 3 * SB_WIDTH,
             3 * SB_WIDTH + MLA_Q_RANK, 3 * SB_WIDTH + MLA_Q_RANK + MLA_KV_RANK)
IN_WIDTH = 3 * SB_WIDTH + MLA_Q_RANK + MLA_KV_RANK + MLA_ROPE_DIM
PEER_HEADS = 8
PEER_N_KEYS = 128
PEER_N_EXPERTS = PEER_N_KEYS * PEER_N_KEYS
PEER_KEY_DIM = 256
PEER_HALF = PEER_KEY_DIM // 2
PEER_TOPK = 16
PEER_TOKEN_BLOCK = 256
ADA_SCALE = 0.5
NORM_EPS = 1e-6

kernel_name = 'sb_mla_peer_streaming_encoder_step'


def rmsnorm(x, g):
    xf = x.astype(jnp.float32)
    y = xf * lax.rsqrt(jnp.mean(xf * xf, axis=-1, keepdims=True) + NORM_EPS)
    return (y * g.astype(jnp.float32)).astype(x.dtype)


def rope(x, pos):
    half = x.shape[-1] // 2
    inv = ROPE_BASE ** (-jnp.arange(half, dtype=jnp.float32) / half)
    ang = pos.astype(jnp.float32)[:, None] * inv[None, :]
    shape = (1, pos.shape[0]) + (1,) * (x.ndim - 3) + (half,)
    cos = jnp.cos(ang).reshape(shape)
    sin = jnp.sin(ang).reshape(shape)
    x1 = x[..., :half].astype(jnp.float32)
    x2 = x[..., half:].astype(jnp.float32)
    return jnp.concatenate([x1 * cos - x2 * sin, x2 * cos + x1 * sin], axis=-1).astype(x.dtype)


def stick_breaking_block(q, k, v, q_pos, k_pos):
    z = jnp.einsum('bqhd,bkhd->bhqk', q, k).astype(jnp.float32) * (SB_HEAD_DIM ** -0.5)
    valid = k_pos[None, :] < q_pos[:, None]
    log_keep = jnp.where(valid, jax.nn.log_sigmoid(-z), 0.0)
    tail = lax.cumsum(log_keep, axis=3, reverse=True) - log_keep
    a = jnp.where(valid, jnp.exp(jax.nn.log_sigmoid(z) + tail), 0.0)
    return jnp.einsum('bhqk,bkhd->bqhd', a.astype(v.dtype), v)


def chunk_causal_softmax(s, q_pos, k_pos):
    visible = (k_pos[None, :] // CHUNK) <= (q_pos[:, None] // CHUNK)
    return jax.nn.softmax(jnp.where(visible, s.astype(jnp.float32) * MLA_SCALE, -jnp.inf), axis=-1)


def peer(h, w_peer_q, sub_keys_1, sub_keys_2, u_experts, v_experts):
    B, T, D = h.shape
    x = h.reshape(B * T, D)
    n = x.shape[0]
    q = (x @ w_peer_q).reshape(n, PEER_HEADS, 2, PEER_HALF)
    s1 = jnp.einsum('nhk,ek->nhe', q[:, :, 0], sub_keys_1).astype(jnp.float32)
    s2 = jnp.einsum('nhk,ek->nhe', q[:, :, 1], sub_keys_2).astype(jnp.float32)
    v1, i1 = lax.top_k(s1, PEER_TOPK)
    v2, i2 = lax.top_k(s2, PEER_TOPK)
    cand_s = (v1[..., :, None] + v2[..., None, :]).reshape(n, PEER_HEADS, PEER_TOPK * PEER_TOPK)
    cand_i = (i1[..., :, None] * PEER_N_KEYS + i2[..., None, :]).reshape(n, PEER_HEADS, PEER_TOPK * PEER_TOPK)
    top_s, top_j = lax.top_k(cand_s, PEER_TOPK)
    idx = jnp.take_along_axis(cand_i, top_j, axis=-1)
    gate = jax.nn.softmax(top_s, axis=-1)
    n_pad = (-n) % PEER_TOKEN_BLOCK
    nb = (n + n_pad) // PEER_TOKEN_BLOCK
    xb = jnp.pad(x, ((0, n_pad), (0, 0))).reshape(nb, PEER_TOKEN_BLOCK, D)
    ib = jnp.pad(idx, ((0, n_pad), (0, 0), (0, 0))).reshape(nb, PEER_TOKEN_BLOCK, PEER_HEADS, PEER_TOPK)
    gb = jnp.pad(gate, ((0, n_pad), (0, 0), (0, 0))).reshape(nb, PEER_TOKEN_BLOCK, PEER_HEADS, PEER_TOPK)

    def token_block(args):
        xt, it, gt = args
        u = u_experts[it]
        act = jax.nn.gelu(jnp.einsum('td,thkd->thk', xt, u).astype(jnp.float32), approximate=False) * gt
        return jnp.einsum('thk,thkd->td', act.astype(xt.dtype), v_experts[it])

    out = lax.map(token_block, (xb, ib, gb)).reshape(nb * PEER_TOKEN_BLOCK, D)[:n]
    return out.reshape(B, T, D)


def hybrid_layer(x, c, pos, past, w_ada, b_ada, g_mix_pre, g_mix_post, g_ffn_pre, g_ffn_post,
                 w_in, g_q_lat, g_kv_lat, w_uq, w_uk, w_uv, g_sb_out, g_mla_out, w_out,
                 w_peer_q, sub_keys_1, sub_keys_2, u_experts, v_experts):
    B, T, _ = x.shape
    ada = jax.nn.silu(c) @ w_ada + b_ada
    sh1, sc1, gt1, sh2, sc2, gt2 = jnp.split(ada[:, None, :], 6, axis=-1)

    h = rmsnorm(x, g_mix_pre) * (1 + sc1) + sh1
    sb_q, sb_k, sb_v, q_lat, ckv, krope = jnp.split(h @ w_in, IN_SPLITS, axis=-1)
    sb_q = sb_q.reshape(B, T, SB_HEADS, SB_HEAD_DIM)
    sb_k = sb_k.reshape(B, T, SB_HEADS, SB_HEAD_DIM)
    sb_v = sb_v.reshape(B, T, SB_HEADS, SB_HEAD_DIM)
    q = (rmsnorm(q_lat, g_q_lat) @ w_uq).reshape(B, T, MLA_HEADS, MLA_QK_DIM)
    q_nope = q[..., :MLA_NOPE_DIM]
    q_rope = rope(q[..., MLA_NOPE_DIM:], pos)
    ckv = rmsnorm(ckv, g_kv_lat)
    krope = rope(krope, pos)

    if past is None:
        k_nope = jnp.einsum('btc,chn->bthn', ckv, w_uk)
        v_mla = jnp.einsum('btc,chv->bthv', ckv, w_uv)
        sb_parts, mla_parts = [], []
        for i in range(T // Q_BLOCK):
            lo, hi = i * Q_BLOCK, (i + 1) * Q_BLOCK
            qp, kp = pos[lo:hi], pos[:hi]
            sb_parts.append(stick_breaking_block(sb_q[:, lo:hi], sb_k[:, :hi], sb_v[:, :hi], qp, kp))
            s = (jnp.einsum('bqhn,bkhn->bhqk', q_nope[:, lo:hi], k_nope[:, :hi])
                 + jnp.einsum('bqhr,bkr->bhqk', q_rope[:, lo:hi], krope[:, :hi]))
            p = chunk_causal_softmax(s, qp, kp)
            mla_parts.append(jnp.einsum('bhqk,bkhv->bqhv', p.astype(v_mla.dtype), v_mla[:, :hi]))
        sb_o = jnp.concatenate(sb_parts, axis=1)
        mla_o = jnp.concatenate(mla_parts, axis=1)
    else:
        c_sb_k, c_sb_v, c_ckv, c_krope = past
        kp = jnp.arange(c_sb_k.shape[1] + T)
        all_sb_k = jnp.concatenate([c_sb_k, sb_k], axis=1)
        all_sb_v = jnp.concatenate([c_sb_v, sb_v], axis=1)
        sb_o = stick_breaking_block(sb_q, all_sb_k, all_sb_v, pos, kp)
        all_ckv = jnp.concatenate([c_ckv, ckv], axis=1)
        all_kr = jnp.concatenate([c_krope, krope], axis=1)
        q_abs = jnp.einsum('bthn,chn->bthc', q_nope, w_uk)
        s = (jnp.einsum('bqhc,bkc->bhqk', q_abs, all_ckv)
             + jnp.einsum('bqhr,bkr->bhqk', q_rope, all_kr))
        p = chunk_causal_softmax(s, pos, kp)
        o_lat = jnp.einsum('bhqk,bkc->bqhc', p.astype(all_ckv.dtype), all_ckv)
        mla_o = jnp.einsum('bqhc,chv->bqhv', o_lat, w_uv)

    merged = jnp.concatenate([rmsnorm(sb_o.reshape(B, T, SB_WIDTH), g_sb_out),
                              rmsnorm(mla_o.reshape(B, T, MLA_WIDTH), g_mla_out)], axis=-1)
    x = x + gt1 * rmsnorm(merged @ w_out, g_mix_post)

    h2 = rmsnorm(x, g_ffn_pre) * (1 + sc2) + sh2
    f = peer(h2, w_peer_q, sub_keys_1, sub_keys_2, u_experts, v_experts)
    x = x + gt2 * rmsnorm(f, g_ffn_post)
    return x, (sb_k, sb_v, ckv, krope)


def setup_inputs(seed: int = 0) -> dict:
    key = jax.random.key(seed)
    ks = iter(jax.random.split(key, 40))
    f32 = jnp.float32

    def nrm(shape, scale):
        return jax.random.normal(next(ks), shape, f32) * scale

    def gain(shape):
        return 1.0 + 0.02 * jax.random.normal(next(ks), shape, f32)

    L, D = DEPTH, D_MODEL
    return {
        'x_prompt': nrm((BATCH, SEQ, D), 1.0),
        'x_sample': nrm((DEC_BATCH, DEC_SEQ, D), 1.0),
        'c_prompt': nrm((BATCH, D), 1.0),
        'c_sample': nrm((DEC_BATCH, D), 1.0),
        'cache_sb_k': nrm((L, DEC_BATCH, PAST_LEN, SB_HEADS, SB_HEAD_DIM), 1.0),
        'cache_sb_v': nrm((L, DEC_BATCH, PAST_LEN, SB_HEADS, SB_HEAD_DIM), 1.0),
        'cache_mla_ckv': nrm((L, DEC_BATCH, PAST_LEN, MLA_KV_RANK), 1.0),
        'cache_mla_krope': nrm((L, DEC_BATCH, PAST_LEN, MLA_ROPE_DIM), 1.0),
        'w_ada': nrm((L, D, 6 * D), ADA_SCALE * D ** -0.5),
        'b_ada': nrm((L, 6 * D), 0.01),
        'g_mix_pre': gain((L, D)),
        'g_mix_post': gain((L, D)),
        'g_ffn_pre': gain((L, D)),
        'g_ffn_post': gain((L, D)),
        'w_in': nrm((L, D, IN_WIDTH), D ** -0.5),
        'g_q_lat': gain((L, MLA_Q_RANK)),
        'g_kv_lat': gain((L, MLA_KV_RANK)),
        'w_uq': nrm((L, MLA_Q_RANK, MLA_HEADS * MLA_QK_DIM), MLA_Q_RANK ** -0.5),
        'w_uk': nrm((L, MLA_KV_RANK, MLA_HEADS, MLA_NOPE_DIM), MLA_KV_RANK ** -0.5),
        'w_uv': nrm((L, MLA_KV_RANK, MLA_HEADS, MLA_V_DIM), MLA_KV_RANK ** -0.5),
        'g_sb_out': gain((L, SB_WIDTH)),
        'g_mla_out': gain((L, MLA_WIDTH)),
        'w_out': nrm((L, MIX_WIDTH, D), MIX_WIDTH ** -0.5),
        'w_peer_q': nrm((L, D, PEER_HEADS * PEER_KEY_DIM), D ** -0.5),
        'sub_keys_1': nrm((L, PEER_N_KEYS, PEER_HALF), PEER_HALF ** -0.5),
        'sub_keys_2': nrm((L, PEER_N_KEYS, PEER_HALF), PEER_HALF ** -0.5),
        'u_experts': nrm((L, PEER_N_EXPERTS, D), D ** -0.5),
        'v_experts': nrm((L, PEER_N_EXPERTS, D), 1.0),
    }


def reference(x_prompt, x_sample, c_prompt, c_sample, cache_sb_k, cache_sb_v, cache_mla_ckv,
              cache_mla_krope, w_ada, b_ada, g_mix_pre, g_mix_post, g_ffn_pre, g_ffn_post,
              w_in, g_q_lat, g_kv_lat, w_uq, w_uk, w_uv, g_sb_out, g_mla_out, w_out,
              w_peer_q, sub_keys_1, sub_keys_2, u_experts, v_experts):
    pos_p = jnp.arange(x_prompt.shape[1])
    pos_s = cache_sb_k.shape[2] + jnp.arange(x_sample.shape[1])
    y_p, y_s = x_prompt, x_sample
    sbk_p, sbv_p, ckv_p, kr_p = [], [], [], []
    sbk_s, sbv_s, ckv_s, kr_s = [], [], [], []
    for l in range(DEPTH):
        lw = (w_ada[l], b_ada[l], g_mix_pre[l], g_mix_post[l], g_ffn_pre[l], g_ffn_post[l],
              w_in[l], g_q_lat[l], g_kv_lat[l], w_uq[l], w_uk[l], w_uv[l], g_sb_out[l],
              g_mla_out[l], w_out[l], w_peer_q[l], sub_keys_1[l], sub_keys_2[l],
              u_experts[l], v_experts[l])
        y_p, (a, b, cc, d) = hybrid_layer(y_p, c_prompt, pos_p, None, *lw)
        sbk_p.append(a); sbv_p.append(b); ckv_p.append(cc); kr_p.append(d)
        past = (cache_sb_k[l], cache_sb_v[l], cache_mla_ckv[l], cache_mla_krope[l])
        y_s, (a, b, cc, d) = hybrid_layer(y_s, c_sample, pos_s, past, *lw)
        sbk_s.append(a); sbv_s.append(b); ckv_s.append(cc); kr_s.append(d)
    return (y_p, y_s,
            jnp.stack(sbk_p), jnp.stack(sbv_p), jnp.stack(ckv_p), jnp.stack(kr_p),
            jnp.stack(sbk_s), jnp.stack(sbv_s), jnp.stack(ckv_s), jnp.stack(kr_s))
```

```python
import functools

import jax
import jax.numpy as jnp
import numpy as np
from jax import lax
from jax.experimental import pallas as pl
from jax.experimental.pallas import tpu as pltpu

F32 = jnp.float32
BF16 = jnp.bfloat16

CHUNK = 64
ROPE_BASE = 10000.0
NORM_EPS = 1e-6
PEER_TOPK = 16
LANES = 128
VMEM_LIMIT = 56 * 1024 * 1024
SB_LOG_FLOOR = -88.0
NT_DIMS = (((1,), (1,)), ((), ()))


def _rms(x, g):
    return x * lax.rsqrt(jnp.mean(x * x, axis=-1, keepdims=True) + NORM_EPS) * g


def _cparams(sem):
    return pltpu.CompilerParams(dimension_semantics=sem, vmem_limit_bytes=VMEM_LIMIT)


def _ada_kernel(c_ref, w_ref, b_ref, o_ref):
    c = c_ref[...]
    s = c * jax.nn.sigmoid(c)
    o_ref[...] = jnp.dot(s.astype(BF16), w_ref[...].astype(BF16), preferred_element_type=F32) + b_ref[...]


def _ada(c_all, w_ada, b_ada):
    rows, d = c_all.shape
    width = w_ada.shape[1]
    return pl.pallas_call(
        _ada_kernel,
        grid=(width // d,),
        in_specs=[pl.BlockSpec((rows, d), lambda j: (0, 0)),
                  pl.BlockSpec((d, d), lambda j: (0, j)),
                  pl.BlockSpec((1, d), lambda j: (0, j))],
        out_specs=pl.BlockSpec((rows, d), lambda j: (0, j)),
        out_shape=jax.ShapeDtypeStruct((rows, width), F32),
        compiler_params=_cparams(("arbitrary",)),
        name="ada",
    )(c_all, w_ada, b_ada.reshape(1, width))


def _in_kernel(x_ref, sc_ref, sh_ref, g_ref, rope_ref, wall_ref, gq_ref, gkv_ref, wuq_ref, wukv_ref,
               sbq_o, sbk_o, sbv_o, sbkb_o, sbvb_o, ckv_o, kr_o, qcat_o, kcat_o, vmla_o,
               *, sbw, qr, kvr, rdim, heads, sb_scale):
    x = x_ref[...]
    h = _rms(x, g_ref[...]) * (1.0 + sc_ref[0]) + sh_ref[0]
    proj = jnp.dot(h.astype(BF16), wall_ref[...], preferred_element_type=F32)
    o = 0
    sbq_o[...] = (proj[:, o:o + sbw] * sb_scale).astype(BF16); o += sbw
    k = proj[:, o:o + sbw]; o += sbw
    v = proj[:, o:o + sbw]; o += sbw
    sbk_o[...] = k
    sbv_o[...] = v
    sbkb_o[...] = k.astype(BF16)
    sbvb_o[...] = v.astype(BF16)
    qlat = proj[:, o:o + qr]; o += qr
    ckv = proj[:, o:o + kvr]; o += kvr
    kr_a = proj[:, o:o + LANES]; o += LANES
    kr_b = proj[:, o:o + LANES]
    rt = rope_ref[...]
    cq, sq = rt[:, 0:LANES], rt[:, LANES:2 * LANES]
    ck, sk = rt[:, 2 * LANES:3 * LANES], rt[:, 3 * LANES:4 * LANES]
    kr128 = kr_a * ck + kr_b * sk
    kr_o[...] = kr128[:, 0:rdim]
    qn = _rms(qlat, gq_ref[...]).astype(BF16)
    q2 = jnp.dot(qn, wuq_ref[...], preferred_element_type=F32)
    gw = heads * LANES
    cq8 = jnp.concatenate([cq] * heads, axis=1)
    sq8 = jnp.concatenate([sq] * heads, axis=1)
    qcat_o[...] = (q2[:, 0:gw] * cq8 + q2[:, gw:2 * gw] * sq8).astype(BF16)
    ckvn = _rms(ckv, gkv_ref[...])
    ckv_o[...] = ckvn
    kv = jnp.dot(ckvn.astype(BF16), wukv_ref[...], preferred_element_type=F32)
    lane = lax.broadcasted_iota(jnp.int32, kr128.shape, 1)
    kr_hi = jnp.where(lane >= 64, kr128, 0.0)
    kcat_o[...] = (kv[:, 0:gw] + jnp.concatenate([kr_hi] * heads, axis=1)).astype(BF16)
    vmla_o[...] = kv[:, gw:].astype(BF16)


def _ada_spec(arr, tm, tiles_per_batch):
    d = arr.shape[-1]
    if arr.shape[1] == 1:
        return pl.BlockSpec((1, 1, d), lambda i: (i // tiles_per_batch, 0, 0))
    return pl.BlockSpec((1, tm, d), lambda i: (0, i, 0))


def _in_proj(x2, sc, sh, g_pre, rope_tab, wts, dims, tm, tiles_per_batch):
    n, d = x2.shape
    sbw, qr, kvr, rdim, heads = dims["sbw"], dims["qr"], dims["kvr"], dims["rdim"], dims["mla_heads"]
    gw = heads * LANES
    vw = wts["w_ukv"].shape[1] - gw
    full = lambda a: pl.BlockSpec(a.shape, lambda i: (0,) * a.ndim)
    row = lambda w: pl.BlockSpec((tm, w), lambda i: (i, 0))
    kern = functools.partial(_in_kernel, sbw=sbw, qr=qr, kvr=kvr, rdim=rdim, heads=heads,
                             sb_scale=dims["sb_dim"] ** -0.5)
    outs = [(sbw, BF16), (sbw, F32), (sbw, F32), (sbw, BF16), (sbw, BF16), (kvr, F32), (rdim, F32),
            (gw, BF16), (gw, BF16), (vw, BF16)]
    return pl.pallas_call(
        kern,
        grid=(n // tm,),
        in_specs=[row(d), _ada_spec(sc, tm, tiles_per_batch), _ada_spec(sh, tm, tiles_per_batch),
                  full(g_pre), pl.BlockSpec((tm, 4 * LANES), lambda i: (i % (rope_tab.shape[0] // tm), 0)),
                  full(wts["w_all"]), full(wts["g_q"]), full(wts["g_kv"]),
                  full(wts["w_uq2"]), full(wts["w_ukv"])],
        out_specs=[row(w) for w, _ in outs],
        out_shape=[jax.ShapeDtypeStruct((n, w), dt) for w, dt in outs],
        compiler_params=_cparams(("arbitrary",)),
        name="in_proj",
    )(x2, sc, sh, g_pre, rope_tab, wts["w_all"], wts["g_q"], wts["g_kv"], wts["w_uq2"], wts["w_ukv"])


def _sb_block(qm, kb, vb, c, acc, mask, tri):
    z = lax.dot_general(qm, kb, NT_DIMS, preferred_element_type=F32)
    sp = jnp.maximum(z, 0.0) + jnp.log(1.0 + jnp.exp(-jnp.abs(z)))
    lk = -sp if mask is None else jnp.where(mask, -sp, 0.0)
    hi = lk.astype(BF16)
    lo = (lk - hi.astype(F32)).astype(BF16)
    incl = jnp.dot(hi, tri, preferred_element_type=F32) + jnp.dot(lo, tri, preferred_element_type=F32)
    e = z - sp + (incl - lk) + c
    a = jnp.exp(e)
    if mask is not None:
        a = jnp.where(mask, a, 0.0)
    acc = acc + jnp.dot(a.astype(BF16), vb, preferred_element_type=F32)
    return c + incl[:, 0:1], acc


def _sb_kernel(q_ref, k_ref, v_ref, o_ref, *, tq, tk, q_off, pairs):
    i = pl.program_id(1)
    q_start = q_off + i * tq
    jm = q_start // tk
    r = lax.broadcasted_iota(jnp.int32, (tk, tk), 0)
    cc = lax.broadcasted_iota(jnp.int32, (tk, tk), 1)
    tri = (r >= cc).astype(BF16)
    qpos = q_start + lax.broadcasted_iota(jnp.int32, (tq, tk), 0)
    kpos = jm * tk + lax.broadcasted_iota(jnp.int32, (tq, tk), 1)
    causal = kpos < qpos
    lane = lax.broadcasted_iota(jnp.int32, (tq, LANES), 1)
    outs = []
    for p in range(pairs):
        sl = slice(p * LANES, (p + 1) * LANES)
        qp = q_ref[0, :, sl]
        halves = []
        for half in range(2):
            in_half = (lane < 64) if half == 0 else (lane >= 64)
            qm = jnp.where(in_half, qp, jnp.zeros_like(qp))

            def step(j, c, acc, mask):
                ks = pl.multiple_of(j * tk, tk)
                return _sb_block(qm, k_ref[0, pl.ds(ks, tk), sl], v_ref[0, pl.ds(ks, tk), sl], c, acc, mask, tri)

            c, acc = step(jm, jnp.zeros((tq, 1), F32), jnp.zeros((tq, LANES), F32), causal)

            def cond(s):
                return s[3] > 0

            def body(s):
                j, c, acc, _ = s
                c, acc = step(j, c, acc, None)
                go = jnp.logical_and(j > 0, jnp.max(c) > SB_LOG_FLOOR)
                return j - 1, c, acc, go.astype(jnp.int32)

            go0 = jnp.logical_and(jm > 0, jnp.max(c) > SB_LOG_FLOOR).astype(jnp.int32)
            _, _, acc, _ = lax.while_loop(cond, body, (jm - 1, c, acc, go0))
            halves.append(acc)
        outs.append(jnp.where(lane < 64, halves[0], halves[1]))
    o_ref[0] = jnp.concatenate(outs, axis=1)


def _sb_attn(q, k, v, tq, tk, q_off):
    b, t_q, w = q.shape
    t_k = k.shape[1]
    assert tk % tq == 0 and q_off % tk == 0 and q_off + t_q <= t_k
    kern = functools.partial(_sb_kernel, tq=tq, tk=tk, q_off=q_off, pairs=w // LANES)
    return pl.pallas_call(
        kern,
        grid=(b, t_q // tq),
        in_specs=[pl.BlockSpec((1, tq, w), lambda bi, i: (bi, i, 0)),
                  pl.BlockSpec((1, t_k, w), lambda bi, i: (bi, 0, 0)),
                  pl.BlockSpec((1, t_k, w), lambda bi, i: (bi, 0, 0))],
        out_specs=pl.BlockSpec((1, tq, w), lambda bi, i: (bi, i, 0)),
        out_shape=jax.ShapeDtypeStruct((b, t_q, w), F32),
        compiler_params=_cparams(("arbitrary", "arbitrary")),
        name="sb_attn",
    )(q, k, v)


def _mla_kernel(q_ref, k_ref, v_ref, o_ref, *, tq, tk, q_off, t_valid, heads):
    i = pl.program_id(1)
    q_start = q_off + i * tq
    jm = q_start // tk
    qpos = q_start + lax.broadcasted_iota(jnp.int32, (tq, tk), 0)
    kpos = jm * tk + lax.broadcasted_iota(jnp.int32, (tq, tk), 1)
    shift = CHUNK.bit_length() - 1
    visible = jnp.logical_and(jnp.right_shift(kpos, shift) <= jnp.right_shift(qpos, shift), kpos < t_valid)
    lane = lax.broadcasted_iota(jnp.int32, (tq, LANES), 1)
    outs = []
    for h in range(heads):
        sl = slice(h * LANES, (h + 1) * LANES)
        vsl = slice((h // 2) * LANES, (h // 2 + 1) * LANES)
        qh = q_ref[0, :, sl]

        def blk(j, m, l, acc, mask):
            ks = pl.multiple_of(j * tk, tk)
            s = lax.dot_general(qh, k_ref[0, pl.ds(ks, tk), sl], NT_DIMS, preferred_element_type=F32)
            if mask is not None:
                s = jnp.where(mask, s, -jnp.inf)
            m_new = jnp.maximum(m, jnp.max(s, axis=1, keepdims=True))
            alpha = jnp.exp(m - m_new)
            p = jnp.exp(s - m_new)
            l = alpha * l + jnp.sum(p, axis=1, keepdims=True)
            acc = alpha * acc + jnp.dot(p.astype(BF16), v_ref[0, pl.ds(ks, tk), vsl],
                                        preferred_element_type=F32)
            return m_new, l, acc

        m, l, acc = blk(jm, jnp.full((tq, 1), -jnp.inf, F32), jnp.zeros((tq, 1), F32),
                        jnp.zeros((tq, LANES), F32), visible)
        m, l, acc = lax.fori_loop(0, jm, lambda j, s: blk(j, *s, None), (m, l, acc))
        outs.append(acc / l)
    merged = [jnp.where(lane < 64, outs[2 * p], outs[2 * p + 1]) for p in range(heads // 2)]
    o_ref[0] = jnp.concatenate(merged, axis=1)


def _mla_attn(q, k, v, tq, tk, q_off, t_valid):
    b, t_q, gw = q.shape
    t_k = k.shape[1]
    vw = v.shape[2]
    assert tk % tq == 0 and q_off % tk == 0 and q_off + t_q <= t_k and tk % CHUNK == 0
    kern = functools.partial(_mla_kernel, tq=tq, tk=tk, q_off=q_off, t_valid=t_valid, heads=gw // LANES)
    return pl.pallas_call(
        kern,
        grid=(b, t_q // tq),
        in_specs=[pl.BlockSpec((1, tq, gw), lambda bi, i: (bi, i, 0)),
                  pl.BlockSpec((1, t_k, gw), lambda bi, i: (bi, 0, 0)),
                  pl.BlockSpec((1, t_k, vw), lambda bi, i: (bi, 0, 0))],
        out_specs=pl.BlockSpec((1, tq, vw), lambda bi, i: (bi, i, 0)),
        out_shape=jax.ShapeDtypeStruct((b, t_q, vw), F32),
        compiler_params=_cparams(("arbitrary", "arbitrary")),
        name="mla_attn",
    )(q, k, v)


def _cache_kv_kernel(ckv_ref, kr_ref, wukv_ref, place_ref, kcat_o, vmla_o, *, heads):
    gw = heads * LANES
    kv = jnp.dot(ckv_ref[...].astype(BF16), wukv_ref[...], preferred_element_type=F32)
    kr_hi = jnp.dot(kr_ref[...].astype(BF16), place_ref[...], preferred_element_type=F32)
    kcat_o[...] = (kv[:, 0:gw] + jnp.concatenate([kr_hi] * heads, axis=1)).astype(BF16)
    vmla_o[...] = kv[:, gw:].astype(BF16)


def _cache_kv(ckv2, kr2, w_ukv, place, heads, tm):
    n, kvr = ckv2.shape
    rdim = kr2.shape[1]
    gw = heads * LANES
    vw = w_ukv.shape[1] - gw
    return pl.pallas_call(
        functools.partial(_cache_kv_kernel, heads=heads),
        grid=(n // tm,),
        in_specs=[pl.BlockSpec((tm, kvr), lambda i: (i, 0)), pl.BlockSpec((tm, rdim), lambda i: (i, 0)),
                  pl.BlockSpec(w_ukv.shape, lambda i: (0, 0)), pl.BlockSpec(place.shape, lambda i: (0, 0))],
        out_specs=[pl.BlockSpec((tm, gw), lambda i: (i, 0)), pl.BlockSpec((tm, vw), lambda i: (i, 0))],
        out_shape=[jax.ShapeDtypeStruct((n, gw), BF16), jax.ShapeDtypeStruct((n, vw), BF16)],
        compiler_params=_cparams(("arbitrary",)),
        name="cache_kv",
    )(ckv2, kr2, w_ukv, place)


def _topk_rows(s, k, payload=None):
    rows = s.shape[0]
    rid = lax.broadcasted_iota(jnp.int32, s.shape, 0).astype(F32)
    vals, picks = [], []
    for _ in range(k):
        m = jnp.max(s, axis=0, keepdims=True)
        first = jnp.min(jnp.where(s == m, rid, float(rows)), axis=0, keepdims=True)
        hit = rid == first
        vals.append(m)
        if payload is None:
            picks.append(first)
        else:
            picks.append(jnp.max(jnp.where(hit, payload, -1.0), axis=0, keepdims=True))
        s = jnp.where(hit, -jnp.inf, s)
    return jnp.concatenate(vals, axis=0), jnp.concatenate(picks, axis=0)


def _staircase(k):
    return [(a, b) for a in range(k) for b in range(k) if (a + 1) * (b + 1) <= k]


def _out_kernel(x_ref, sbo_ref, mlao_ref, gt1_ref, sc2_ref, sh2_ref, gsb_ref, gmla_ref, gpost_ref, gffn_ref,
                wout_ref, wpq_ref, k1_ref, k2_ref, x1_o, h2_o, idx_o, gate_o, q_scr, *, peer_heads, n_keys):
    sb = _rms(sbo_ref[...], gsb_ref[...])
    ml = _rms(mlao_ref[...], gmla_ref[...])
    merged = jnp.concatenate([sb, ml], axis=1).astype(BF16)
    y = jnp.dot(merged, wout_ref[...], preferred_element_type=F32)
    x1 = x_ref[...] + gt1_ref[0] * _rms(y, gpost_ref[...])
    x1_o[...] = x1
    h2 = _rms(x1, gffn_ref[...]) * (1.0 + sc2_ref[0]) + sh2_ref[0]
    h2_o[...] = h2
    q = jnp.dot(h2.astype(BF16), wpq_ref[...], preferred_element_type=F32)
    half = q.shape[1] // (2 * peer_heads)
    for hd in range(peer_heads):
        for side in range(2):
            o = (2 * hd + side) * half
            q_scr[hd, side] = q[:, o:o + half].astype(BF16)
    pairs = _staircase(PEER_TOPK)
    k1 = k1_ref[...]
    k2 = k2_ref[...]

    def per_head(hd, carry):
        s1 = lax.dot_general(k1, q_scr[hd, 0], NT_DIMS, preferred_element_type=F32)
        s2 = lax.dot_general(k2, q_scr[hd, 1], NT_DIMS, preferred_element_type=F32)
        v1, i1 = _topk_rows(s1, PEER_TOPK)
        v2, i2 = _topk_rows(s2, PEER_TOPK)
        fill = (-len(pairs)) % 8
        tokens = v1.shape[1]
        cs = jnp.concatenate([v1[a:a + 1] + v2[b:b + 1] for a, b in pairs]
                             + [jnp.full((fill, tokens), -jnp.inf, F32)], axis=0)
        ci = jnp.concatenate([i1[a:a + 1] * float(n_keys) + i2[b:b + 1] for a, b in pairs]
                             + [jnp.full((fill, tokens), -1.0, F32)], axis=0)
        ts, ti = _topk_rows(cs, PEER_TOPK, payload=ci)
        ex = jnp.exp(ts - ts[0:1])
        gate_o[hd] = ex / jnp.sum(ex, axis=0, keepdims=True)
        idx_o[hd] = ti.astype(jnp.int32)
        return carry

    lax.fori_loop(0, peer_heads, per_head, 0)


def _out_peer(x2, sbo, mlao, gt1, sc2, sh2, wts, dims, tm, tiles_per_batch):
    n, d = x2.shape
    ph, nk = dims["peer_heads"], dims["n_keys"]
    half = wts["w_pq"].shape[1] // (2 * ph)
    full = lambda a: pl.BlockSpec(a.shape, lambda i: (0,) * a.ndim)
    row = lambda w: pl.BlockSpec((tm, w), lambda i: (i, 0))
    ada = lambda a: _ada_spec(a, tm, tiles_per_batch)
    names = ["g_sb", "g_mla", "g_post", "g_ffn", "w_out", "w_pq", "k1", "k2"]
    return pl.pallas_call(
        functools.partial(_out_kernel, peer_heads=ph, n_keys=nk),
        grid=(n // tm,),
        in_specs=[row(d), row(sbo.shape[1]), row(mlao.shape[1]), ada(gt1), ada(sc2), ada(sh2)]
                 + [full(wts[k]) for k in names],
        out_specs=[row(d), row(d),
                   pl.BlockSpec((ph, PEER_TOPK, tm), lambda i: (0, 0, i)),
                   pl.BlockSpec((ph, PEER_TOPK, tm), lambda i: (0, 0, i))],
        out_shape=[jax.ShapeDtypeStruct((n, d), F32), jax.ShapeDtypeStruct((n, d), F32),
                   jax.ShapeDtypeStruct((ph, PEER_TOPK, n), jnp.int32),
                   jax.ShapeDtypeStruct((ph, PEER_TOPK, n), F32)],
        scratch_shapes=[pltpu.VMEM((ph, 2, tm, half), BF16)],
        compiler_params=_cparams(("arbitrary",)),
        name="out_peer",
    )(x2, sbo, mlao, gt1, sc2, sh2, *[wts[k] for k in names])


def _gelu(x):
    return 0.5 * x * (1.0 + lax.erf(x * (2.0 ** -0.5)))


def _peer_kernel(idx_hbm, h2_ref, gate_ref, x1_ref, gt2_ref, gpost_ref, tab_hbm, y_o,
                 idx_smem, buf, f_scr, idx_sem, row_sem, *, tt, picks, chunks, nbuf):
    tile = pl.program_id(0)
    idx_cp = pltpu.make_async_copy(idx_hbm.at[tile], idx_smem, idx_sem)
    idx_cp.start()
    idx_cp.wait()

    def row_copy(e, slot, k):
        return pltpu.make_async_copy(tab_hbm.at[e], buf.at[slot, :, k, :], row_sem.at[slot])

    def issue(t, slot):
        def one(k, carry):
            row_copy(idx_smem[t * picks + k], slot, k).start()
            return carry
        lax.fori_loop(0, picks, one, 0, unroll=8)

    def wait_all(slot):
        def one(k, carry):
            row_copy(0, slot, k).wait()
            return carry
        lax.fori_loop(0, picks, one, 0, unroll=8)

    for t0 in range(nbuf - 1):
        issue(t0, t0)

    eye = (lax.broadcasted_iota(jnp.int32, (picks, picks), 0)
           == lax.broadcasted_iota(jnp.int32, (picks, picks), 1))

    def per_token(t, carry):
        slot = t % nbuf

        @pl.when(t + nbuf - 1 < tt)
        def _():
            issue(t + nbuf - 1, (t + nbuf - 1) % nbuf)

        wait_all(slot)
        xt = h2_ref[pl.ds(t, 1), :]
        part = jnp.zeros((picks, LANES), F32)
        for c in range(chunks):
            part = part + buf[slot, c] * xt[:, c * LANES:(c + 1) * LANES]
        s = jnp.sum(part, axis=1, keepdims=True)
        g_col = jnp.sum(jnp.where(eye, gate_ref[pl.ds(t, 1), :], 0.0), axis=1, keepdims=True)
        a = _gelu(s) * g_col
        outs = [jnp.sum(a * buf[slot, chunks + c], axis=0, keepdims=True) for c in range(chunks)]
        f_scr[pl.ds(t, 1), :] = jnp.concatenate(outs, axis=1)
        return carry

    lax.fori_loop(0, tt, per_token, 0)
    y_o[...] = x1_ref[...] + gt2_ref[0] * _rms(f_scr[...], gpost_ref[...])


def _peer_ffn(idx_tiles, h2, gate, x1, gt2, g_post, table, tt, tiles_per_batch, nbuf=3):
    n, d = h2.shape
    picks = gate.shape[1]
    chunks = d // LANES
    assert table.shape[1:] == (2 * chunks, LANES)
    kern = functools.partial(_peer_kernel, tt=tt, picks=picks, chunks=chunks, nbuf=nbuf)
    row = lambda w: pl.BlockSpec((tt, w), lambda i: (i, 0))
    return pl.pallas_call(
        kern,
        grid=(n // tt,),
        in_specs=[pl.BlockSpec(memory_space=pl.ANY), row(d), row(picks), row(d),
                  _ada_spec(gt2, tt, tiles_per_batch), pl.BlockSpec(g_post.shape, lambda i: (0, 0)),
                  pl.BlockSpec(memory_space=pl.ANY)],
        out_specs=row(d),
        out_shape=jax.ShapeDtypeStruct((n, d), F32),
        scratch_shapes=[pltpu.SMEM((tt * picks,), jnp.int32),
                        pltpu.VMEM((nbuf, 2 * chunks, picks, LANES), F32),
                        pltpu.VMEM((tt, d), F32),
                        pltpu.SemaphoreType.DMA(()),
                        pltpu.SemaphoreType.DMA((nbuf,))],
        compiler_params=_cparams(("arbitrary",)),
        name="peer_ffn",
    )(idx_tiles, h2, gate, x1, gt2, g_post, table)


def _rope_table(pos, rdim, q_scale):
    half = rdim // 2
    inv = ROPE_BASE ** (-jnp.arange(half, dtype=F32) / half)
    ang = pos.astype(F32)[:, None] * inv[None, :]
    cos = jnp.concatenate([jnp.cos(ang), jnp.cos(ang)], axis=1)
    sin = jnp.concatenate([-jnp.sin(ang), jnp.sin(ang)], axis=1)
    t = pos.shape[0]
    z = lambda w: jnp.zeros((t, w), F32)
    cq = jnp.concatenate([jnp.ones((t, 64), F32), cos, z(64 - rdim)], axis=1) * q_scale
    sq = jnp.concatenate([z(64), sin, z(64 - rdim)], axis=1) * q_scale
    ck = jnp.concatenate([cos, z(64 - rdim), cos, z(64 - rdim)], axis=1)
    sk = jnp.concatenate([sin, z(64 - rdim), sin, z(64 - rdim)], axis=1)
    return jnp.concatenate([cq, sq, ck, sk], axis=1)


def _swap_halves(w):
    half = w.shape[-1] // 2
    return jnp.concatenate([w[..., half:], w[..., :half]], axis=-1)


def _prep_weights(w_in, g_q_lat, g_kv_lat, w_uq, w_uk, w_uv, dims):
    d = w_in.shape[0]
    sbw, qr, kvr, rdim = dims["sbw"], dims["qr"], dims["kvr"], dims["rdim"]
    heads, nope, vdim = dims["mla_heads"], dims["nope"], dims["vdim"]
    assert nope == 64 and rdim <= 64 and vdim == 64 and dims["sb_dim"] == 64
    o = 3 * sbw
    w_kr = w_in[:, o + qr + kvr:]
    zk = jnp.zeros((d, 64 - rdim), F32)
    kr_a = jnp.concatenate([w_kr, zk, w_kr, zk], axis=1)
    kr_b = jnp.concatenate([_swap_halves(w_kr), zk, _swap_halves(w_kr), zk], axis=1)
    w_all = jnp.concatenate([w_in[:, :o + qr + kvr], kr_a, kr_b], axis=1).astype(BF16)
    uq = w_uq.reshape(qr, heads, nope + rdim)
    zq = jnp.zeros((qr, heads, LANES - nope - rdim), F32)
    uq_a = jnp.concatenate([uq, zq], axis=2)
    uq_b = jnp.concatenate([jnp.zeros((qr, heads, nope), F32), _swap_halves(uq[..., nope:]), zq], axis=2)
    w_uq2 = jnp.concatenate([uq_a.reshape(qr, -1), uq_b.reshape(qr, -1)], axis=1).astype(BF16)
    uk = jnp.concatenate([w_uk, jnp.zeros((kvr, heads, LANES - nope), F32)], axis=2).reshape(kvr, -1)
    w_ukv = jnp.concatenate([uk, w_uv.reshape(kvr, -1)], axis=1).astype(BF16)
    place = (jnp.arange(rdim)[:, None] + 64 == jnp.arange(LANES)[None, :]).astype(BF16)
    return dict(w_all=w_all, w_uq2=w_uq2, w_ukv=w_ukv, place=place,
                g_q=g_q_lat.reshape(1, -1), g_kv=g_kv_lat.reshape(1, -1))


def _layer(x, ada, pos0, past, lw, dims, tq_sb, tq_mla, tk_sb, tk_mla, tm_in, tm_out, tt):
    b, t, d = x.shape
    n = b * t
    x2 = x.reshape(n, d)
    per_token = t < tm_in
    if per_token:
        sh1, sc1, gt1, sh2, sc2, gt2 = [jnp.repeat(a, t, axis=0)[None] for a in ada]
        tiles_in = tiles_out = tiles_tt = 1
    else:
        sh1, sc1, gt1, sh2, sc2, gt2 = [a[:, None, :] for a in ada]
        tiles_in, tiles_out, tiles_tt = t // tm_in, t // tm_out, t // tt
    tm_in, tm_out, tt = min(tm_in, n), min(tm_out, n), min(tt, n)
    heads = dims["mla_heads"]
    q_scale = (dims["nope"] + dims["rdim"]) ** -0.5
    rope_tab = _rope_table(pos0 + jnp.arange(t), dims["rdim"], q_scale)
    rope_tab = jnp.tile(rope_tab, (b, 1)) if per_token else rope_tab
    (sbq, sbk, sbv, sbkb, sbvb, ckvn, kr, qcat, kcat, vmla) = _in_proj(
        x2, sc1, sh1, lw["g_mix_pre"], rope_tab, lw, dims, tm_in, tiles_in)
    sbw, gw, vw = dims["sbw"], heads * LANES, vmla.shape[1]
    r3 = lambda a: a.reshape(b, t, a.shape[-1])
    if past is None:
        q_off, t_valid = 0, t
        k_sb, v_sb, k_m, v_m = r3(sbkb), r3(sbvb), r3(kcat), r3(vmla)
        q_sb, q_m = r3(sbq), r3(qcat)
        tq_s, tq_m = tq_sb, tq_mla
    else:
        c_k, c_v, c_ckv, c_kr = past
        p_len = c_k.shape[1]
        q_off, t_valid = p_len, p_len + t
        tq_s = tq_m = t
        kc, vc = _cache_kv(c_ckv.reshape(b * p_len, -1), c_kr.reshape(b * p_len, -1), lw["w_ukv"],
                           lw["place"], heads, tm_in)

        def with_cache(cache, new, tk):
            pad = (-(p_len + t)) % tk
            return jnp.concatenate([cache, new, jnp.zeros((b, pad, new.shape[-1]), new.dtype)], axis=1)

        k_sb = with_cache(c_k.reshape(b, p_len, sbw).astype(BF16), r3(sbkb), tk_sb)
        v_sb = with_cache(c_v.reshape(b, p_len, sbw).astype(BF16), r3(sbvb), tk_sb)
        k_m = with_cache(kc.reshape(b, p_len, gw), r3(kcat), tk_mla)
        v_m = with_cache(vc.reshape(b, p_len, vw), r3(vmla), tk_mla)
        q_sb, q_m = r3(sbq), r3(qcat)
    sbo = _sb_attn(q_sb, k_sb, v_sb, tq_s, tk_sb, q_off)
    mlao = _mla_attn(q_m, k_m, v_m, tq_m, tk_mla, q_off, t_valid)
    x1, h2, idx, gate = _out_peer(x2, sbo.reshape(n, sbw), mlao.reshape(n, vw), gt1, sc2, sh2, lw, dims,
                                  tm_out, tiles_out)
    picks = dims["peer_heads"] * PEER_TOPK
    idx_tiles = idx.reshape(picks, n).T.reshape(n // tt, tt * picks)
    gate_rows = gate.reshape(picks, n).T
    y = _peer_ffn(idx_tiles, h2, gate_rows, x1, gt2, lw["g_post_ffn"], lw["table"], tt, tiles_tt)
    new = (sbk.reshape(b, t, dims["sb_heads"], dims["sb_dim"]), sbv.reshape(b, t, dims["sb_heads"], dims["sb_dim"]),
           ckvn.reshape(b, t, -1), kr.reshape(b, t, -1))
    return y.reshape(b, t, d), new


def kernel(x_prompt, x_sample, c_prompt, c_sample, cache_sb_k, cache_sb_v, cache_mla_ckv, cache_mla_krope,
           w_ada, b_ada, g_mix_pre, g_mix_post, g_ffn_pre, g_ffn_post, w_in, g_q_lat, g_kv_lat, w_uq, w_uk,
           w_uv, g_sb_out, g_mla_out, w_out, w_peer_q, sub_keys_1, sub_keys_2, u_experts, v_experts):
    depth = w_ada.shape[0]
    d = x_prompt.shape[-1]
    sb_heads, sb_dim = cache_sb_k.shape[-2:]
    kvr, mla_heads, nope = w_uk.shape[1:]
    dims = dict(sb_heads=sb_heads, sb_dim=sb_dim, sbw=sb_heads * sb_dim, qr=w_uq.shape[1], kvr=kvr,
                rdim=cache_mla_krope.shape[-1], mla_heads=mla_heads, nope=nope, vdim=w_uv.shape[-1],
                n_keys=sub_keys_1.shape[1], peer_heads=w_peer_q.shape[2] // (2 * sub_keys_1.shape[2]))
    bp, bs = c_prompt.shape[0], c_sample.shape[0]
    pad = (-(bp + bs)) % 8
    y_p, y_s = x_prompt, x_sample
    outs_p, outs_s = [], []
    for l in range(depth):
        c_all = jnp.concatenate([c_prompt, c_sample, jnp.zeros((pad, d), F32)], axis=0)
        ada = _ada(c_all, w_ada[l], b_ada[l])
        ada_p = jnp.split(ada[:bp], 6, axis=1)
        ada_s = jnp.split(ada[bp:bp + bs], 6, axis=1)
        lw = _prep_weights(w_in[l], g_q_lat[l], g_kv_lat[l], w_uq[l], w_uk[l], w_uv[l], dims)
        e = u_experts.shape[1]
        lw.update(
            g_mix_pre=g_mix_pre[l].reshape(1, d), g_post=g_mix_post[l].reshape(1, d),
            g_ffn=g_ffn_pre[l].reshape(1, d), g_post_ffn=g_ffn_post[l].reshape(1, d),
            g_sb=g_sb_out[l].reshape(1, -1), g_mla=g_mla_out[l].reshape(1, -1),
            w_out=w_out[l].astype(BF16), w_pq=w_peer_q[l].astype(BF16),
            k1=sub_keys_1[l].astype(BF16), k2=sub_keys_2[l].astype(BF16),
            table=jnp.concatenate([u_experts[l].reshape(e, d // LANES, LANES),
                                   v_experts[l].reshape(e, d // LANES, LANES)], axis=1))
        tiles = dict(tq_sb=256, tq_mla=512, tk_sb=256, tk_mla=512, tm_in=512, tm_out=256, tt=64)
        y_p, new_p = _layer(y_p, ada_p, 0, None, lw, dims, **tiles)
        past = (cache_sb_k[l], cache_sb_v[l], cache_mla_ckv[l], cache_mla_krope[l])
        y_s, new_s = _layer(y_s, ada_s, cache_sb_k.shape[2], past, lw, dims, **tiles)
        outs_p.append(new_p)
        outs_s.append(new_s)
    stack = lambda outs, k: jnp.stack([o[k] for o in outs])
    return (y_p, y_s, stack(outs_p, 0), stack(outs_p, 1), stack(outs_p, 2), stack(outs_p, 3),
            stack(outs_s, 0), stack(outs_s, 1), stack(outs_s, 2), stack(outs_s, 3))
```

```python
import functools

import jax
import jax.numpy as jnp
import numpy as np
from jax import lax
from jax.experimental import pallas as pl
from jax.experimental.pallas import tpu as pltpu

F32 = jnp.float32
BF16 = jnp.bfloat16

CHUNK = 64
ROPE_BASE = 10000.0
NORM_EPS = 1e-6
PEER_TOPK = 16
LANES = 128
VMEM_LIMIT = 56 * 1024 * 1024
SB_LOG_FLOOR = -88.0
NT_DIMS = (((1,), (1,)), ((), ()))


def _rms(x, g):
    return x * lax.rsqrt(jnp.mean(x * x, axis=-1, keepdims=True) + NORM_EPS) * g


def _cparams(sem):
    return pltpu.CompilerParams(dimension_semantics=sem, vmem_limit_bytes=VMEM_LIMIT)


def _ada_kernel(c_ref, w_ref, b_ref, o_ref):
    c = c_ref[...]
    s = c * jax.nn.sigmoid(c)
    o_ref[...] = jnp.dot(s.astype(BF16), w_ref[...].astype(BF16), preferred_element_type=F32) + b_ref[...]


def _ada(c_all, w_ada, b_ada):
    rows, d = c_all.shape
    width = w_ada.shape[1]
    return pl.pallas_call(
        _ada_kernel,
        grid=(width // d,),
        in_specs=[pl.BlockSpec((rows, d), lambda j: (0, 0)),
                  pl.BlockSpec((d, d), lambda j: (0, j)),
                  pl.BlockSpec((1, d), lambda j: (0, j))],
        out_specs=pl.BlockSpec((rows, d), lambda j: (0, j)),
        out_shape=jax.ShapeDtypeStruct((rows, width), F32),
        compiler_params=_cparams(("arbitrary",)),
        name="ada",
    )(c_all, w_ada, b_ada.reshape(1, width))


def _in_kernel(x_ref, sc_ref, sh_ref, g_ref, rope_ref, wall_ref, gq_ref, gkv_ref, wuq_ref, wukv_ref,
               sbq_o, sbk_o, sbv_o, sbkb_o, sbvb_o, ckv_o, kr_o, qcat_o, kcat_o, vmla_o,
               *, sbw, qr, kvr, rdim, heads, sb_scale):
    x = x_ref[...]
    h = _rms(x, g_ref[...]) * (1.0 + sc_ref[0]) + sh_ref[0]
    proj = jnp.dot(h.astype(BF16), wall_ref[...], preferred_element_type=F32)
    o = 0
    sbq_o[...] = (proj[:, o:o + sbw] * sb_scale).astype(BF16); o += sbw
    k = proj[:, o:o + sbw]; o += sbw
    v = proj[:, o:o + sbw]; o += sbw
    sbk_o[...] = k
    sbv_o[...] = v
    sbkb_o[...] = k.astype(BF16)
    sbvb_o[...] = v.astype(BF16)
    qlat = proj[:, o:o + qr]; o += qr
    ckv = proj[:, o:o + kvr]; o += kvr
    kr_a = proj[:, o:o + LANES]; o += LANES
    kr_b = proj[:, o:o + LANES]
    rt = rope_ref[...]
    cq, sq = rt[:, 0:LANES], rt[:, LANES:2 * LANES]
    ck, sk = rt[:, 2 * LANES:3 * LANES], rt[:, 3 * LANES:4 * LANES]
    kr128 = kr_a * ck + kr_b * sk
    kr_o[...] = kr128[:, 0:rdim]
    qn = _rms(qlat, gq_ref[...]).astype(BF16)
    q2 = jnp.dot(qn, wuq_ref[...], preferred_element_type=F32)
    gw = heads * LANES
    cq8 = jnp.concatenate([cq] * heads, axis=1)
    sq8 = jnp.concatenate([sq] * heads, axis=1)
    qcat_o[...] = (q2[:, 0:gw] * cq8 + q2[:, gw:2 * gw] * sq8).astype(BF16)
    ckvn = _rms(ckv, gkv_ref[...])
    ckv_o[...] = ckvn
    kv = jnp.dot(ckvn.astype(BF16), wukv_ref[...], preferred_element_type=F32)
    lane = lax.broadcasted_iota(jnp.int32, kr128.shape, 1)
    kr_hi = jnp.where(lane >= 64, kr128, 0.0)
    kcat_o[...] = (kv[:, 0:gw] + jnp.concatenate([kr_hi] * heads, axis=1)).astype(BF16)
    vmla_o[...] = kv[:, gw:].astype(BF16)


def _ada_spec(arr, tm, tiles_per_batch):
    d = arr.shape[-1]
    if arr.shape[1] == 1:
        return pl.BlockSpec((1, 1, d), lambda i: (i // tiles_per_batch, 0, 0))
    return pl.BlockSpec((1, tm, d), lambda i: (0, i, 0))


def _in_proj(x2, sc, sh, g_pre, rope_tab, wts, dims, tm, tiles_per_batch):
    n, d = x2.shape
    sbw, qr, kvr, rdim, heads = dims["sbw"], dims["qr"], dims["kvr"], dims["rdim"], dims["mla_heads"]
    gw = heads * LANES
    vw = wts["w_ukv"].shape[1] - gw
    full = lambda a: pl.BlockSpec(a.shape, lambda i: (0,) * a.ndim)
    row = lambda w: pl.BlockSpec((tm, w), lambda i: (i, 0))
    kern = functools.partial(_in_kernel, sbw=sbw, qr=qr, kvr=kvr, rdim=rdim, heads=heads,
                             sb_scale=dims["sb_dim"] ** -0.5)
    outs = [(sbw, BF16), (sbw, F32), (sbw, F32), (sbw, BF16), (sbw, BF16), (kvr, F32), (rdim, F32),
            (gw, BF16), (gw, BF16), (vw, BF16)]
    return pl.pallas_call(
        kern,
        grid=(n // tm,),
        in_specs=[row(d), _ada_spec(sc, tm, tiles_per_batch), _ada_spec(sh, tm, tiles_per_batch),
                  full(g_pre), pl.BlockSpec((tm, 4 * LANES), lambda i: (i % (rope_tab.shape[0] // tm), 0)),
                  full(wts["w_all"]), full(wts["g_q"]), full(wts["g_kv"]),
                  full(wts["w_uq2"]), full(wts["w_ukv"])],
        out_specs=[row(w) for w, _ in outs],
        out_shape=[jax.ShapeDtypeStruct((n, w), dt) for w, dt in outs],
        compiler_params=_cparams(("arbitrary",)),
        name="in_proj",
    )(x2, sc, sh, g_pre, rope_tab, wts["w_all"], wts["g_q"], wts["g_kv"], wts["w_uq2"], wts["w_ukv"])


def _sb_block(qm, kb, vb, c, acc, mask, tri):
    z = lax.dot_general(qm, kb, NT_DIMS, preferred_element_type=F32)
    sp = jnp.maximum(z, 0.0) + jnp.log(1.0 + jnp.exp(-jnp.abs(z)))
    lk = -sp if mask is None else jnp.where(mask, -sp, 0.0)
    hi = lk.astype(BF16)
    lo = (lk - hi.astype(F32)).astype(BF16)
    incl = jnp.dot(hi, tri, preferred_element_type=F32) + jnp.dot(lo, tri, preferred_element_type=F32)
    e = z - sp + (incl - lk) + c
    a = jnp.exp(e)
    if mask is not None:
        a = jnp.where(mask, a, 0.0)
    acc = acc + jnp.dot(a.astype(BF16), vb, preferred_element_type=F32)
    return c + incl[:, 0:1], acc


def _sb_kernel(q_ref, k_ref, v_ref, o_ref, *, tq, tk, q_off, pairs):
    i = pl.program_id(1)
    q_start = q_off + i * tq
    jm = q_start // tk
    r = lax.broadcasted_iota(jnp.int32, (tk, tk), 0)
    cc = lax.broadcasted_iota(jnp.int32, (tk, tk), 1)
    tri = (r >= cc).astype(BF16)
    qpos = q_start + lax.broadcasted_iota(jnp.int32, (tq, tk), 0)
    kpos = jm * tk + lax.broadcasted_iota(jnp.int32, (tq, tk), 1)
    causal = kpos < qpos
    lane = lax.broadcasted_iota(jnp.int32, (tq, LANES), 1)
    outs = []
    for p in range(pairs):
        sl = slice(p * LANES, (p + 1) * LANES)
        qp = q_ref[0, :, sl]
        halves = []
        for half in range(2):
            in_half = (lane < 64) if half == 0 else (lane >= 64)
            qm = jnp.where(in_half, qp, jnp.zeros_like(qp))

            def step(j, c, acc, mask):
                ks = pl.multiple_of(j * tk, tk)
                return _sb_block(qm, k_ref[0, pl.ds(ks, tk), sl], v_ref[0, pl.ds(ks, tk), sl], c, acc, mask, tri)

            c, acc = step(jm, jnp.zeros((tq, 1), F32), jnp.zeros((tq, LANES), F32), causal)

            def cond(s):
                return s[3] > 0

            def body(s):
                j, c, acc, _ = s
                c, acc = step(j, c, acc, None)
                go = jnp.logical_and(j > 0, jnp.max(c) > SB_LOG_FLOOR)
                return j - 1, c, acc, go.astype(jnp.int32)

            go0 = jnp.logical_and(jm > 0, jnp.max(c) > SB_LOG_FLOOR).astype(jnp.int32)
            _, _, acc, _ = lax.while_loop(cond, body, (jm - 1, c, acc, go0))
            halves.append(acc)
        outs.append(jnp.where(lane < 64, halves[0], halves[1]))
    o_ref[0] = jnp.concatenate(outs, axis=1)


def _sb_attn(q, k, v, tq, tk, q_off):
    b, t_q, w = q.shape
    t_k = k.shape[1]
    assert tk % tq == 0 and q_off % tk == 0 and q_off + t_q <= t_k
    kern = functools.partial(_sb_kernel, tq=tq, tk=tk, q_off=q_off, pairs=w // LANES)
    return pl.pallas_call(
        kern,
        grid=(b, t_q // tq),
        in_specs=[pl.BlockSpec((1, tq, w), lambda bi, i: (bi, i, 0)),
                  pl.BlockSpec((1, t_k, w), lambda bi, i: (bi, 0, 0)),
                  pl.BlockSpec((1, t_k, w), lambda bi, i: (bi, 0, 0))],
        out_specs=pl.BlockSpec((1, tq, w), lambda bi, i: (bi, i, 0)),
        out_shape=jax.ShapeDtypeStruct((b, t_q, w), F32),
        compiler_params=_cparams(("arbitrary", "arbitrary")),
        name="sb_attn",
    )(q, k, v)


def _mla_kernel(q_ref, k_ref, v_ref, o_ref, *, tq, tk, q_off, t_valid, heads):
    i = pl.program_id(1)
    q_start = q_off + i * tq
    jm = q_start // tk
    qpos = q_start + lax.broadcasted_iota(jnp.int32, (tq, tk), 0)
    kpos = jm * tk + lax.broadcasted_iota(jnp.int32, (tq, tk), 1)
    shift = CHUNK.bit_length() - 1
    visible = jnp.logical_and(jnp.right_shift(kpos, shift) <= jnp.right_shift(qpos, shift), kpos < t_valid)
    lane = lax.broadcasted_iota(jnp.int32, (tq, LANES), 1)
    outs = []
    for h in range(heads):
        sl = slice(h * LANES, (h + 1) * LANES)
        vsl = slice((h // 2) * LANES, (h // 2 + 1) * LANES)
        qh = q_ref[0, :, sl]

        def blk(j, m, l, acc, mask):
            ks = pl.multiple_of(j * tk, tk)
            s = lax.dot_general(qh, k_ref[0, pl.ds(ks, tk), sl], NT_DIMS, preferred_element_type=F32)
            if mask is not None:
                s = jnp.where(mask, s, -jnp.inf)
            m_new = jnp.maximum(m, jnp.max(s, axis=1, keepdims=True))
            alpha = jnp.exp(m - m_new)
            p = jnp.exp(s - m_new)
            l = alpha * l + jnp.sum(p, axis=1, keepdims=True)
            acc = alpha * acc + jnp.dot(p.astype(BF16), v_ref[0, pl.ds(ks, tk), vsl],
                                        preferred_element_type=F32)
            return m_new, l, acc

        m, l, acc = blk(jm, jnp.full((tq, 1), -jnp.inf, F32), jnp.zeros((tq, 1), F32),
                        jnp.zeros((tq, LANES), F32), visible)
        m, l, acc = lax.fori_loop(0, jm, lambda j, s: blk(j, *s, None), (m, l, acc))
        outs.append(acc / l)
    merged = [jnp.where(lane < 64, outs[2 * p], outs[2 * p + 1]) for p in range(heads // 2)]
    o_ref[0] = jnp.concatenate(merged, axis=1)


def _mla_attn(q, k, v, tq, tk, q_off, t_valid):
    b, t_q, gw = q.shape
    t_k = k.shape[1]
    vw = v.shape[2]
    assert tk % tq == 0 and q_off % tk == 0 and q_off + t_q <= t_k and tk % CHUNK == 0
    kern = functools.partial(_mla_kernel, tq=tq, tk=tk, q_off=q_off, t_valid=t_valid, heads=gw // LANES)
    return pl.pallas_call(
        kern,
        grid=(b, t_q // tq),
        in_specs=[pl.BlockSpec((1, tq, gw), lambda bi, i: (bi, i, 0)),
                  pl.BlockSpec((1, t_k, gw), lambda bi, i: (bi, 0, 0)),
                  pl.BlockSpec((1, t_k, vw), lambda bi, i: (bi, 0, 0))],
        out_specs=pl.BlockSpec((1, tq, vw), lambda bi, i: (bi, i, 0)),
        out_shape=jax.ShapeDtypeStruct((b, t_q, vw), F32),
        compiler_params=_cparams(("arbitrary", "arbitrary")),
        name="mla_attn",
    )(q, k, v)


def _cache_kv_kernel(ckv_ref, kr_ref, wukv_ref, place_ref, kcat_o, vmla_o, *, heads):
    gw = heads * LANES
    kv = jnp.dot(ckv_ref[...].astype(BF16), wukv_ref[...], preferred_element_type=F32)
    kr_hi = jnp.dot(kr_ref[...].astype(BF16), place_ref[...], preferred_element_type=F32)
    kcat_o[...] = (kv[:, 0:gw] + jnp.concatenate([kr_hi] * heads, axis=1)).astype(BF16)
    vmla_o[...] = kv[:, gw:].astype(BF16)


def _cache_kv(ckv2, kr2, w_ukv, place, heads, tm):
    n, kvr = ckv2.shape
    rdim = kr2.shape[1]
    gw = heads * LANES
    vw = w_ukv.shape[1] - gw
    return pl.pallas_call(
        functools.partial(_cache_kv_kernel, heads=heads),
        grid=(n // tm,),
        in_specs=[pl.BlockSpec((tm, kvr), lambda i: (i, 0)), pl.BlockSpec((tm, rdim), lambda i: (i, 0)),
                  pl.BlockSpec(w_ukv.shape, lambda i: (0, 0)), pl.BlockSpec(place.shape, lambda i: (0, 0))],
        out_specs=[pl.BlockSpec((tm, gw), lambda i: (i, 0)), pl.BlockSpec((tm, vw), lambda i: (i, 0))],
        out_shape=[jax.ShapeDtypeStruct((n, gw), BF16), jax.ShapeDtypeStruct((n, vw), BF16)],
        compiler_params=_cparams(("arbitrary",)),
        name="cache_kv",
    )(ckv2, kr2, w_ukv, place)


def _topk_rows(s, k, payload=None):
    rows = s.shape[0]
    rid = lax.broadcasted_iota(jnp.int32, s.shape, 0).astype(F32)
    vals, picks = [], []
    for _ in range(k):
        m = jnp.max(s, axis=0, keepdims=True)
        first = jnp.min(jnp.where(s == m, rid, float(rows)), axis=0, keepdims=True)
        hit = rid == first
        vals.append(m)
        if payload is None:
            picks.append(first)
        else:
            picks.append(jnp.max(jnp.where(hit, payload, -1.0), axis=0, keepdims=True))
        s = jnp.where(hit, -jnp.inf, s)
    return jnp.concatenate(vals, axis=0), jnp.concatenate(picks, axis=0)


def _staircase(k):
    return [(a, b) for a in range(k) for b in range(k) if (a + 1) * (b + 1) <= k]


def _out_kernel(x_ref, sbo_ref, mlao_ref, gt1_ref, sc2_ref, sh2_ref, gsb_ref, gmla_ref, gpost_ref, gffn_ref,
                wout_ref, wpq_ref, k1_ref, k2_ref, x1_o, h2_o, idx_o, gate_o, q_scr, *, peer_heads, n_keys):
    sb = _rms(sbo_ref[...], gsb_ref[...])
    ml = _rms(mlao_ref[...], gmla_ref[...])
    merged = jnp.concatenate([sb, ml], axis=1).astype(BF16)
    y = jnp.dot(merged, wout_ref[...], preferred_element_type=F32)
    x1 = x_ref[...] + gt1_ref[0] * _rms(y, gpost_ref[...])
    x1_o[...] = x1
    h2 = _rms(x1, gffn_ref[...]) * (1.0 + sc2_ref[0]) + sh2_ref[0]
    h2_o[...] = h2
    q = jnp.dot(h2.astype(BF16), wpq_ref[...], preferred_element_type=F32)
    half = q.shape[1] // (2 * peer_heads)
    for hd in range(peer_heads):
        for side in range(2):
            o = (2 * hd + side) * half
            q_scr[hd, side] = q[:, o:o + half].astype(BF16)
    pairs = _staircase(PEER_TOPK)
    k1 = k1_ref[...]
    k2 = k2_ref[...]

    def per_head(hd, carry):
        s1 = lax.dot_general(k1, q_scr[hd, 0], NT_DIMS, preferred_element_type=F32)
        s2 = lax.dot_general(k2, q_scr[hd, 1], NT_DIMS, preferred_element_type=F32)
        v1, i1 = _topk_rows(s1, PEER_TOPK)
        v2, i2 = _topk_rows(s2, PEER_TOPK)
        fill = (-len(pairs)) % 8
        tokens = v1.shape[1]
        cs = jnp.concatenate([v1[a:a + 1] + v2[b:b + 1] for a, b in pairs]
                             + [jnp.full((fill, tokens), -jnp.inf, F32)], axis=0)
        ci = jnp.concatenate([i1[a:a + 1] * float(n_keys) + i2[b:b + 1] for a, b in pairs]
                             + [jnp.full((fill, tokens), -1.0, F32)], axis=0)
        ts, ti = _topk_rows(cs, PEER_TOPK, payload=ci)
        ex = jnp.exp(ts - ts[0:1])
        gate_o[hd] = ex / jnp.sum(ex, axis=0, keepdims=True)
        idx_o[hd] = ti.astype(jnp.int32)
        return carry

    lax.fori_loop(0, peer_heads, per_head, 0)


def _out_peer(x2, sbo, mlao, gt1, sc2, sh2, wts, dims, tm, tiles_per_batch):
    n, d = x2.shape
    ph, nk = dims["peer_heads"], dims["n_keys"]
    half = wts["w_pq"].shape[1] // (2 * ph)
    full = lambda a: pl.BlockSpec(a.shape, lambda i: (0,) * a.ndim)
    row = lambda w: pl.BlockSpec((tm, w), lambda i: (i, 0))
    ada = lambda a: _ada_spec(a, tm, tiles_per_batch)
    names = ["g_sb", "g_mla", "g_post", "g_ffn", "w_out", "w_pq", "k1", "k2"]
    return pl.pallas_call(
        functools.partial(_out_kernel, peer_heads=ph, n_keys=nk),
        grid=(n // tm,),
        in_specs=[row(d), row(sbo.shape[1]), row(mlao.shape[1]), ada(gt1), ada(sc2), ada(sh2)]
                 + [full(wts[k]) for k in names],
        out_specs=[row(d), row(d),
                   pl.BlockSpec((ph, PEER_TOPK, tm), lambda i: (0, 0, i)),
                   pl.BlockSpec((ph, PEER_TOPK, tm), lambda i: (0, 0, i))],
        out_shape=[jax.ShapeDtypeStruct((n, d), F32), jax.ShapeDtypeStruct((n, d), F32),
                   jax.ShapeDtypeStruct((ph, PEER_TOPK, n), jnp.int32),
                   jax.ShapeDtypeStruct((ph, PEER_TOPK, n), F32)],
        scratch_shapes=[pltpu.VMEM((ph, 2, tm, half), BF16)],
        compiler_params=_cparams(("arbitrary",)),
        name="out_peer",
    )(x2, sbo, mlao, gt1, sc2, sh2, *[wts[k] for k in names])


def _gelu(x):
    return 0.5 * x * (1.0 + lax.erf(x * (2.0 ** -0.5)))


def _split_bf16(x):
    hi = x.astype(BF16)
    return hi, (x - hi.astype(F32)).astype(BF16)


def _peer_kernel(idx_hbm, slot_hbm, tab_hbm, h2_ref, gate_ref, x1_ref, gt2_ref, gpost_ref, y_o,
                 idx_smem, gate_scr, f_scr, idx_sem, row_sem, *bufs,
                 tt, picks, chunks, grp, nsets, ntiles):
    tile = pl.program_id(0)
    par = tile % 2
    ahead = (nsets - 1) * grp
    n_it = tt // (grp * nsets)

    def idx_copy(tl, half):
        return pltpu.make_async_copy(idx_hbm.at[tl], idx_smem.at[half], idx_sem.at[half])

    def row_copy(e, slot, k):
        return pltpu.make_async_copy(tab_hbm.at[e], bufs[slot].at[:, k, :], row_sem.at[slot])

    def issue(half, t, slot):
        for k in range(picks):
            row_copy(idx_smem[half, k, t], slot, k).start(priority=k % 2)

    def wait_slot(slot):
        pltpu.make_async_copy(slot_hbm.at[0], bufs[slot], row_sem.at[slot]).wait()

    @pl.when(tile == 0)
    def _():
        first = idx_copy(0, 0)
        first.start()
        first.wait()
        for t in range(ahead):
            def one(k, carry, t=t):
                row_copy(idx_smem[0, k, t], t, k).start()
                return carry
            lax.fori_loop(0, picks, one, 0)

    nxt = idx_copy((tile + 1) % ntiles, 1 - par)
    nxt.start()
    gate_scr[...] = gate_ref[...].T

    ones8 = jnp.ones((8, LANES), BF16)
    eye = (lax.broadcasted_iota(jnp.int32, (picks, picks), 0)
           == lax.broadcasted_iota(jnp.int32, (picks, picks), 1)).astype(BF16)

    def compute(slot, t):
        buf = bufs[slot]
        xt = h2_ref[pl.ds(t, 1), :]
        part = buf[0] * xt[:, 0:LANES]
        for c in range(1, chunks):
            part = part + buf[c] * xt[:, c * LANES:(c + 1) * LANES]
        hi, lo = _split_bf16(part)
        s8 = (lax.dot_general(ones8, hi, NT_DIMS, preferred_element_type=F32)
              + lax.dot_general(ones8, lo, NT_DIMS, preferred_element_type=F32))
        a8 = _gelu(s8) * gate_scr[pl.ds(t, 1), :]
        ah, al = _split_bf16(a8)
        reps = picks // 8
        acol = (lax.dot_general(eye, jnp.concatenate([ah] * reps, axis=0), NT_DIMS, preferred_element_type=F32)
                + lax.dot_general(eye, jnp.concatenate([al] * reps, axis=0), NT_DIMS, preferred_element_type=F32))
        outs = [jnp.sum(acol * buf[chunks + c], axis=0, keepdims=True) for c in range(chunks)]
        f_scr[pl.ds(t, 1), :] = jnp.concatenate(outs, axis=1)

    def body(it, carry):
        @pl.when(it == n_it - 1)
        def _():
            nxt.wait()

        for s in range(nsets):
            t0 = (it * nsets + s) * grp
            for j in range(grp):
                wait_slot(s * grp + j)
            for j in range(grp):
                la = t0 + j + ahead
                cross = la >= tt
                issue(jnp.where(cross, 1 - par, par), jnp.where(cross, la - tt, la),
                      ((s + nsets - 1) % nsets) * grp + j)
                compute(s * grp + j, t0 + j)
        return carry

    lax.fori_loop(0, n_it, body, 0)

    @pl.when(tile == ntiles - 1)
    def _():
        for slot in range(ahead):
            wait_slot(slot)

    y_o[...] = x1_ref[...] + gt2_ref[0] * _rms(f_scr[...], gpost_ref[...])


def _peer_ffn(idx, h2, gate, x1, gt2, g_post, table, tt, tiles_per_batch, grp=4, nsets=4):
    n, d = h2.shape
    picks = gate.shape[0]
    chunks = d // LANES
    ntiles = n // tt
    assert table.shape[1:] == (2 * chunks, LANES) and tt % (grp * nsets) == 0 and picks % 8 == 0
    assert table.shape[0] % picks == 0
    idx_tiles = idx.reshape(picks, ntiles, tt).transpose(1, 0, 2)
    slot_view = table.reshape(-1, 2 * chunks, picks, LANES)
    kern = functools.partial(_peer_kernel, tt=tt, picks=picks, chunks=chunks, grp=grp, nsets=nsets,
                             ntiles=ntiles)
    row = lambda w: pl.BlockSpec((tt, w), lambda i: (i, 0))
    hbm = pl.BlockSpec(memory_space=pl.ANY)
    return pl.pallas_call(
        kern,
        grid=(ntiles,),
        in_specs=[hbm, hbm, hbm, row(d), pl.BlockSpec((picks, tt), lambda i: (0, i)), row(d),
                  _ada_spec(gt2, tt, tiles_per_batch), pl.BlockSpec(g_post.shape, lambda i: (0, 0))],
        out_specs=row(d),
        out_shape=jax.ShapeDtypeStruct((n, d), F32),
        scratch_shapes=[pltpu.SMEM((2, picks, tt), jnp.int32),
                        pltpu.VMEM((tt, picks), F32),
                        pltpu.VMEM((tt, d), F32),
                        pltpu.SemaphoreType.DMA((2,)),
                        pltpu.SemaphoreType.DMA((grp * nsets,))]
                       + [pltpu.VMEM((2 * chunks, picks, LANES), F32) for _ in range(grp * nsets)],
        compiler_params=_cparams(("arbitrary",)),
        name="peer_ffn",
    )(idx_tiles, slot_view, table, h2, gate, x1, gt2, g_post)


def _rope_table(pos, rdim, q_scale):
    half = rdim // 2
    inv = ROPE_BASE ** (-jnp.arange(half, dtype=F32) / half)
    ang = pos.astype(F32)[:, None] * inv[None, :]
    cos = jnp.concatenate([jnp.cos(ang), jnp.cos(ang)], axis=1)
    sin = jnp.concatenate([-jnp.sin(ang), jnp.sin(ang)], axis=1)
    t = pos.shape[0]
    z = lambda w: jnp.zeros((t, w), F32)
    cq = jnp.concatenate([jnp.ones((t, 64), F32), cos, z(64 - rdim)], axis=1) * q_scale
    sq = jnp.concatenate([z(64), sin, z(64 - rdim)], axis=1) * q_scale
    ck = jnp.concatenate([cos, z(64 - rdim), cos, z(64 - rdim)], axis=1)
    sk = jnp.concatenate([sin, z(64 - rdim), sin, z(64 - rdim)], axis=1)
    return jnp.concatenate([cq, sq, ck, sk], axis=1)


def _swap_halves(w):
    half = w.shape[-1] // 2
    return jnp.concatenate([w[..., half:], w[..., :half]], axis=-1)


def _prep_weights(w_in, g_q_lat, g_kv_lat, w_uq, w_uk, w_uv, dims):
    d = w_in.shape[0]
    sbw, qr, kvr, rdim = dims["sbw"], dims["qr"], dims["kvr"], dims["rdim"]
    heads, nope, vdim = dims["mla_heads"], dims["nope"], dims["vdim"]
    assert nope == 64 and rdim <= 64 and vdim == 64 and dims["sb_dim"] == 64
    o = 3 * sbw
    w_kr = w_in[:, o + qr + kvr:]
    zk = jnp.zeros((d, 64 - rdim), F32)
    kr_a = jnp.concatenate([w_kr, zk, w_kr, zk], axis=1)
    kr_b = jnp.concatenate([_swap_halves(w_kr), zk, _swap_halves(w_kr), zk], axis=1)
    w_all = jnp.concatenate([w_in[:, :o + qr + kvr], kr_a, kr_b], axis=1).astype(BF16)
    uq = w_uq.reshape(qr, heads, nope + rdim)
    zq = jnp.zeros((qr, heads, LANES - nope - rdim), F32)
    uq_a = jnp.concatenate([uq, zq], axis=2)
    uq_b = jnp.concatenate([jnp.zeros((qr, heads, nope), F32), _swap_halves(uq[..., nope:]), zq], axis=2)
    w_uq2 = jnp.concatenate([uq_a.reshape(qr, -1), uq_b.reshape(qr, -1)], axis=1).astype(BF16)
    uk = jnp.concatenate([w_uk, jnp.zeros((kvr, heads, LANES - nope), F32)], axis=2).reshape(kvr, -1)
    w_ukv = jnp.concatenate([uk, w_uv.reshape(kvr, -1)], axis=1).astype(BF16)
    place = (jnp.arange(rdim)[:, None] + 64 == jnp.arange(LANES)[None, :]).astype(BF16)
    return dict(w_all=w_all, w_uq2=w_uq2, w_ukv=w_ukv, place=place,
                g_q=g_q_lat.reshape(1, -1), g_kv=g_kv_lat.reshape(1, -1))


def _layer(x, ada, pos0, past, lw, dims, tq_sb, tq_mla, tk_sb, tk_mla, tm_in, tm_out, tt):
    b, t, d = x.shape
    n = b * t
    x2 = x.reshape(n, d)
    per_token = t < tm_in
    if per_token:
        sh1, sc1, gt1, sh2, sc2, gt2 = [jnp.repeat(a, t, axis=0)[None] for a in ada]
        tiles_in = tiles_out = tiles_tt = 1
    else:
        sh1, sc1, gt1, sh2, sc2, gt2 = [a[:, None, :] for a in ada]
        tiles_in, tiles_out, tiles_tt = t // tm_in, t // tm_out, t // tt
    tm_in, tm_out, tt = min(tm_in, n), min(tm_out, n), min(tt, n)
    heads = dims["mla_heads"]
    q_scale = (dims["nope"] + dims["rdim"]) ** -0.5
    rope_tab = _rope_table(pos0 + jnp.arange(t), dims["rdim"], q_scale)
    rope_tab = jnp.tile(rope_tab, (b, 1)) if per_token else rope_tab
    (sbq, sbk, sbv, sbkb, sbvb, ckvn, kr, qcat, kcat, vmla) = _in_proj(
        x2, sc1, sh1, lw["g_mix_pre"], rope_tab, lw, dims, tm_in, tiles_in)
    sbw, gw, vw = dims["sbw"], heads * LANES, vmla.shape[1]
    r3 = lambda a: a.reshape(b, t, a.shape[-1])
    if past is None:
        q_off, t_valid = 0, t
        k_sb, v_sb, k_m, v_m = r3(sbkb), r3(sbvb), r3(kcat), r3(vmla)
        q_sb, q_m = r3(sbq), r3(qcat)
        tq_s, tq_m = tq_sb, tq_mla
    else:
        c_k, c_v, c_ckv, c_kr = past
        p_len = c_k.shape[1]
        q_off, t_valid = p_len, p_len + t
        tq_s = tq_m = t
        kc, vc = _cache_kv(c_ckv.reshape(b * p_len, -1), c_kr.reshape(b * p_len, -1), lw["w_ukv"],
                           lw["place"], heads, tm_in)

        def with_cache(cache, new, tk):
            pad = (-(p_len + t)) % tk
            return jnp.concatenate([cache, new, jnp.zeros((b, pad, new.shape[-1]), new.dtype)], axis=1)

        k_sb = with_cache(c_k.reshape(b, p_len, sbw).astype(BF16), r3(sbkb), tk_sb)
        v_sb = with_cache(c_v.reshape(b, p_len, sbw).astype(BF16), r3(sbvb), tk_sb)
        k_m = with_cache(kc.reshape(b, p_len, gw), r3(kcat), tk_mla)
        v_m = with_cache(vc.reshape(b, p_len, vw), r3(vmla), tk_mla)
        q_sb, q_m = r3(sbq), r3(qcat)
    sbo = _sb_attn(q_sb, k_sb, v_sb, tq_s, tk_sb, q_off)
    mlao = _mla_attn(q_m, k_m, v_m, tq_m, tk_mla, q_off, t_valid)
    x1, h2, idx, gate = _out_peer(x2, sbo.reshape(n, sbw), mlao.reshape(n, vw), gt1, sc2, sh2, lw, dims,
                                  tm_out, tiles_out)
    picks = dims["peer_heads"] * PEER_TOPK
    y = _peer_ffn(idx.reshape(picks, n), h2, gate.reshape(picks, n), x1, gt2, lw["g_post_ffn"], lw["table"],
                  tt, tiles_tt)
    new = (sbk.reshape(b, t, dims["sb_heads"], dims["sb_dim"]), sbv.reshape(b, t, dims["sb_heads"], dims["sb_dim"]),
           ckvn.reshape(b, t, -1), kr.reshape(b, t, -1))
    return y.reshape(b, t, d), new


def kernel(x_prompt, x_sample, c_prompt, c_sample, cache_sb_k, cache_sb_v, cache_mla_ckv, cache_mla_krope,
           w_ada, b_ada, g_mix_pre, g_mix_post, g_ffn_pre, g_ffn_post, w_in, g_q_lat, g_kv_lat, w_uq, w_uk,
           w_uv, g_sb_out, g_mla_out, w_out, w_peer_q, sub_keys_1, sub_keys_2, u_experts, v_experts):
    depth = w_ada.shape[0]
    d = x_prompt.shape[-1]
    sb_heads, sb_dim = cache_sb_k.shape[-2:]
    kvr, mla_heads, nope = w_uk.shape[1:]
    dims = dict(sb_heads=sb_heads, sb_dim=sb_dim, sbw=sb_heads * sb_dim, qr=w_uq.shape[1], kvr=kvr,
                rdim=cache_mla_krope.shape[-1], mla_heads=mla_heads, nope=nope, vdim=w_uv.shape[-1],
                n_keys=sub_keys_1.shape[1], peer_heads=w_peer_q.shape[2] // (2 * sub_keys_1.shape[2]))
    bp, bs = c_prompt.shape[0], c_sample.shape[0]
    pad = (-(bp + bs)) % 8
    y_p, y_s = x_prompt, x_sample
    outs_p, outs_s = [], []
    for l in range(depth):
        c_all = jnp.concatenate([c_prompt, c_sample, jnp.zeros((pad, d), F32)], axis=0)
        ada = _ada(c_all, w_ada[l], b_ada[l])
        ada_p = jnp.split(ada[:bp], 6, axis=1)
        ada_s = jnp.split(ada[bp:bp + bs], 6, axis=1)
        lw = _prep_weights(w_in[l], g_q_lat[l], g_kv_lat[l], w_uq[l], w_uk[l], w_uv[l], dims)
        e = u_experts.shape[1]
        lw.update(
            g_mix_pre=g_mix_pre[l].reshape(1, d), g_post=g_mix_post[l].reshape(1, d),
            g_ffn=g_ffn_pre[l].reshape(1, d), g_post_ffn=g_ffn_post[l].reshape(1, d),
            g_sb=g_sb_out[l].reshape(1, -1), g_mla=g_mla_out[l].reshape(1, -1),
            w_out=w_out[l].astype(BF16), w_pq=w_peer_q[l].astype(BF16),
            k1=sub_keys_1[l].astype(BF16), k2=sub_keys_2[l].astype(BF16),
            table=jnp.concatenate([u_experts[l].reshape(e, d // LANES, LANES),
                                   v_experts[l].reshape(e, d // LANES, LANES)], axis=1))
        tiles = dict(tq_sb=256, tq_mla=512, tk_sb=256, tk_mla=512, tm_in=512, tm_out=256, tt=256)
        y_p, new_p = _layer(y_p, ada_p, 0, None, lw, dims, **tiles)
        past = (cache_sb_k[l], cache_sb_v[l], cache_mla_ckv[l], cache_mla_krope[l])
        y_s, new_s = _layer(y_s, ada_s, cache_sb_k.shape[2], past, lw, dims, **tiles)
        outs_p.append(new_p)
        outs_s.append(new_s)
    stack = lambda outs, k: jnp.stack([o[k] for o in outs])
    return (y_p, y_s, stack(outs_p, 0), stack(outs_p, 1), stack(outs_p, 2), stack(outs_p, 3),
            stack(outs_s, 0), stack(outs_s, 1), stack(outs_s, 2), stack(outs_s, 3))
```

```python
import functools

import jax
import jax.numpy as jnp
import numpy as np
from jax import lax
from jax.experimental import pallas as pl
from jax.experimental.pallas import tpu as pltpu

F32 = jnp.float32
BF16 = jnp.bfloat16

CHUNK = 64
ROPE_BASE = 10000.0
NORM_EPS = 1e-6
PEER_TOPK = 16
LANES = 128
VMEM_LIMIT = 56 * 1024 * 1024
SB_LOG_FLOOR = -88.0
NT_DIMS = (((1,), (1,)), ((), ()))


def _rms(x, g):
    return x * lax.rsqrt(jnp.mean(x * x, axis=-1, keepdims=True) + NORM_EPS) * g


def _cparams(sem):
    return pltpu.CompilerParams(dimension_semantics=sem, vmem_limit_bytes=VMEM_LIMIT)


def _ada_kernel(c_ref, w_ref, b_ref, o_ref):
    c = c_ref[...]
    s = c * jax.nn.sigmoid(c)
    o_ref[...] = jnp.dot(s.astype(BF16), w_ref[...].astype(BF16), preferred_element_type=F32) + b_ref[...]


def _ada(c_all, w_ada, b_ada):
    rows, d = c_all.shape
    width = w_ada.shape[1]
    return pl.pallas_call(
        _ada_kernel,
        grid=(width // d,),
        in_specs=[pl.BlockSpec((rows, d), lambda j: (0, 0)),
                  pl.BlockSpec((d, d), lambda j: (0, j)),
                  pl.BlockSpec((1, d), lambda j: (0, j))],
        out_specs=pl.BlockSpec((rows, d), lambda j: (0, j)),
        out_shape=jax.ShapeDtypeStruct((rows, width), F32),
        compiler_params=_cparams(("arbitrary",)),
        name="ada",
    )(c_all, w_ada, b_ada.reshape(1, width))


def _in_kernel(x_ref, sc_ref, sh_ref, g_ref, rope_ref, wall_ref, gq_ref, gkv_ref, wuq_ref, wukv_ref,
               sbq_o, sbk_o, sbv_o, sbkb_o, sbvb_o, ckv_o, kr_o, qcat_o, kcat_o, vmla_o,
               *, sbw, qr, kvr, rdim, heads, sb_scale):
    x = x_ref[...]
    h = _rms(x, g_ref[...]) * (1.0 + sc_ref[0]) + sh_ref[0]
    proj = jnp.dot(h.astype(BF16), wall_ref[...], preferred_element_type=F32)
    o = 0
    sbq_o[...] = (proj[:, o:o + sbw] * sb_scale).astype(BF16); o += sbw
    k = proj[:, o:o + sbw]; o += sbw
    v = proj[:, o:o + sbw]; o += sbw
    sbk_o[...] = k
    sbv_o[...] = v
    sbkb_o[...] = k.astype(BF16)
    sbvb_o[...] = v.astype(BF16)
    qlat = proj[:, o:o + qr]; o += qr
    ckv = proj[:, o:o + kvr]; o += kvr
    kr_a = proj[:, o:o + LANES]; o += LANES
    kr_b = proj[:, o:o + LANES]
    rt = rope_ref[...]
    cq, sq = rt[:, 0:LANES], rt[:, LANES:2 * LANES]
    ck, sk = rt[:, 2 * LANES:3 * LANES], rt[:, 3 * LANES:4 * LANES]
    kr128 = kr_a * ck + kr_b * sk
    kr_o[...] = kr128[:, 0:rdim]
    qn = _rms(qlat, gq_ref[...]).astype(BF16)
    q2 = jnp.dot(qn, wuq_ref[...], preferred_element_type=F32)
    gw = heads * LANES
    cq8 = jnp.concatenate([cq] * heads, axis=1)
    sq8 = jnp.concatenate([sq] * heads, axis=1)
    qcat_o[...] = (q2[:, 0:gw] * cq8 + q2[:, gw:2 * gw] * sq8).astype(BF16)
    ckvn = _rms(ckv, gkv_ref[...])
    ckv_o[...] = ckvn
    kv = jnp.dot(ckvn.astype(BF16), wukv_ref[...], preferred_element_type=F32)
    lane = lax.broadcasted_iota(jnp.int32, kr128.shape, 1)
    kr_hi = jnp.where(lane >= 64, kr128, 0.0)
    kcat_o[...] = (kv[:, 0:gw] + jnp.concatenate([kr_hi] * heads, axis=1)).astype(BF16)
    vmla_o[...] = kv[:, gw:].astype(BF16)


def _ada_spec(arr, tm, tiles_per_batch):
    d = arr.shape[-1]
    if arr.shape[1] == 1:
        return pl.BlockSpec((1, 1, d), lambda i: (i // tiles_per_batch, 0, 0))
    return pl.BlockSpec((1, tm, d), lambda i: (0, i, 0))


def _in_proj(x2, sc, sh, g_pre, rope_tab, wts, dims, tm, tiles_per_batch):
    n, d = x2.shape
    sbw, qr, kvr, rdim, heads = dims["sbw"], dims["qr"], dims["kvr"], dims["rdim"], dims["mla_heads"]
    gw = heads * LANES
    vw = wts["w_ukv"].shape[1] - gw
    full = lambda a: pl.BlockSpec(a.shape, lambda i: (0,) * a.ndim)
    row = lambda w: pl.BlockSpec((tm, w), lambda i: (i, 0))
    kern = functools.partial(_in_kernel, sbw=sbw, qr=qr, kvr=kvr, rdim=rdim, heads=heads,
                             sb_scale=dims["sb_dim"] ** -0.5)
    outs = [(sbw, BF16), (sbw, F32), (sbw, F32), (sbw, BF16), (sbw, BF16), (kvr, F32), (rdim, F32),
            (gw, BF16), (gw, BF16), (vw, BF16)]
    return pl.pallas_call(
        kern,
        grid=(n // tm,),
        in_specs=[row(d), _ada_spec(sc, tm, tiles_per_batch), _ada_spec(sh, tm, tiles_per_batch),
                  full(g_pre), pl.BlockSpec((tm, 4 * LANES), lambda i: (i % (rope_tab.shape[0] // tm), 0)),
                  full(wts["w_all"]), full(wts["g_q"]), full(wts["g_kv"]),
                  full(wts["w_uq2"]), full(wts["w_ukv"])],
        out_specs=[row(w) for w, _ in outs],
        out_shape=[jax.ShapeDtypeStruct((n, w), dt) for w, dt in outs],
        compiler_params=_cparams(("arbitrary",)),
        name="in_proj",
    )(x2, sc, sh, g_pre, rope_tab, wts["w_all"], wts["g_q"], wts["g_kv"], wts["w_uq2"], wts["w_ukv"])


def _sb_block(qm, kb, vb, c, acc, mask, tri):
    z = lax.dot_general(qm, kb, NT_DIMS, preferred_element_type=F32)
    sp = jnp.maximum(z, 0.0) + jnp.log(1.0 + jnp.exp(-jnp.abs(z)))
    lk = -sp if mask is None else jnp.where(mask, -sp, 0.0)
    hi = lk.astype(BF16)
    lo = (lk - hi.astype(F32)).astype(BF16)
    incl = jnp.dot(hi, tri, preferred_element_type=F32) + jnp.dot(lo, tri, preferred_element_type=F32)
    e = z - sp + (incl - lk) + c
    a = jnp.exp(e)
    if mask is not None:
        a = jnp.where(mask, a, 0.0)
    acc = acc + jnp.dot(a.astype(BF16), vb, preferred_element_type=F32)
    return c + incl[:, 0:1], acc


def _sb_kernel(q_ref, k_ref, v_ref, o_ref, *, tq, tk, q_off, pairs):
    i = pl.program_id(1)
    q_start = q_off + i * tq
    jm = q_start // tk
    r = lax.broadcasted_iota(jnp.int32, (tk, tk), 0)
    cc = lax.broadcasted_iota(jnp.int32, (tk, tk), 1)
    tri = (r >= cc).astype(BF16)
    qpos = q_start + lax.broadcasted_iota(jnp.int32, (tq, tk), 0)
    kpos = jm * tk + lax.broadcasted_iota(jnp.int32, (tq, tk), 1)
    causal = kpos < qpos
    lane = lax.broadcasted_iota(jnp.int32, (tq, LANES), 1)
    outs = []
    for p in range(pairs):
        sl = slice(p * LANES, (p + 1) * LANES)
        qp = q_ref[0, :, sl]
        halves = []
        for half in range(2):
            in_half = (lane < 64) if half == 0 else (lane >= 64)
            qm = jnp.where(in_half, qp, jnp.zeros_like(qp))

            def step(j, c, acc, mask):
                ks = pl.multiple_of(j * tk, tk)
                return _sb_block(qm, k_ref[0, pl.ds(ks, tk), sl], v_ref[0, pl.ds(ks, tk), sl], c, acc, mask, tri)

            c, acc = step(jm, jnp.zeros((tq, 1), F32), jnp.zeros((tq, LANES), F32), causal)

            def cond(s):
                return s[3] > 0

            def body(s):
                j, c, acc, _ = s
                c, acc = step(j, c, acc, None)
                go = jnp.logical_and(j > 0, jnp.max(c) > SB_LOG_FLOOR)
                return j - 1, c, acc, go.astype(jnp.int32)

            go0 = jnp.logical_and(jm > 0, jnp.max(c) > SB_LOG_FLOOR).astype(jnp.int32)
            _, _, acc, _ = lax.while_loop(cond, body, (jm - 1, c, acc, go0))
            halves.append(acc)
        outs.append(jnp.where(lane < 64, halves[0], halves[1]))
    o_ref[0] = jnp.concatenate(outs, axis=1)


def _sb_attn(q, k, v, tq, tk, q_off):
    b, t_q, w = q.shape
    t_k = k.shape[1]
    assert tk % tq == 0 and q_off % tk == 0 and q_off + t_q <= t_k
    kern = functools.partial(_sb_kernel, tq=tq, tk=tk, q_off=q_off, pairs=w // LANES)
    return pl.pallas_call(
        kern,
        grid=(b, t_q // tq),
        in_specs=[pl.BlockSpec((1, tq, w), lambda bi, i: (bi, i, 0)),
                  pl.BlockSpec((1, t_k, w), lambda bi, i: (bi, 0, 0)),
                  pl.BlockSpec((1, t_k, w), lambda bi, i: (bi, 0, 0))],
        out_specs=pl.BlockSpec((1, tq, w), lambda bi, i: (bi, i, 0)),
        out_shape=jax.ShapeDtypeStruct((b, t_q, w), F32),
        compiler_params=_cparams(("arbitrary", "arbitrary")),
        name="sb_attn",
    )(q, k, v)


def _mla_kernel(q_ref, k_ref, v_ref, o_ref, *, tq, tk, q_off, t_valid, heads):
    i = pl.program_id(1)
    q_start = q_off + i * tq
    jm = q_start // tk
    qpos = q_start + lax.broadcasted_iota(jnp.int32, (tq, tk), 0)
    kpos = jm * tk + lax.broadcasted_iota(jnp.int32, (tq, tk), 1)
    shift = CHUNK.bit_length() - 1
    visible = jnp.logical_and(jnp.right_shift(kpos, shift) <= jnp.right_shift(qpos, shift), kpos < t_valid)
    lane = lax.broadcasted_iota(jnp.int32, (tq, LANES), 1)
    outs = []
    for h in range(heads):
        sl = slice(h * LANES, (h + 1) * LANES)
        vsl = slice((h // 2) * LANES, (h // 2 + 1) * LANES)
        qh = q_ref[0, :, sl]

        def blk(j, m, l, acc, mask):
            ks = pl.multiple_of(j * tk, tk)
            s = lax.dot_general(qh, k_ref[0, pl.ds(ks, tk), sl], NT_DIMS, preferred_element_type=F32)
            if mask is not None:
                s = jnp.where(mask, s, -jnp.inf)
            m_new = jnp.maximum(m, jnp.max(s, axis=1, keepdims=True))
            alpha = jnp.exp(m - m_new)
            p = jnp.exp(s - m_new)
            l = alpha * l + jnp.sum(p, axis=1, keepdims=True)
            acc = alpha * acc + jnp.dot(p.astype(BF16), v_ref[0, pl.ds(ks, tk), vsl],
                                        preferred_element_type=F32)
            return m_new, l, acc

        m, l, acc = blk(jm, jnp.full((tq, 1), -jnp.inf, F32), jnp.zeros((tq, 1), F32),
                        jnp.zeros((tq, LANES), F32), visible)
        m, l, acc = lax.fori_loop(0, jm, lambda j, s: blk(j, *s, None), (m, l, acc))
        outs.append(acc / l)
    merged = [jnp.where(lane < 64, outs[2 * p], outs[2 * p + 1]) for p in range(heads // 2)]
    o_ref[0] = jnp.concatenate(merged, axis=1)


def _mla_attn(q, k, v, tq, tk, q_off, t_valid):
    b, t_q, gw = q.shape
    t_k = k.shape[1]
    vw = v.shape[2]
    assert tk % tq == 0 and q_off % tk == 0 and q_off + t_q <= t_k and tk % CHUNK == 0
    kern = functools.partial(_mla_kernel, tq=tq, tk=tk, q_off=q_off, t_valid=t_valid, heads=gw // LANES)
    return pl.pallas_call(
        kern,
        grid=(b, t_q // tq),
        in_specs=[pl.BlockSpec((1, tq, gw), lambda bi, i: (bi, i, 0)),
                  pl.BlockSpec((1, t_k, gw), lambda bi, i: (bi, 0, 0)),
                  pl.BlockSpec((1, t_k, vw), lambda bi, i: (bi, 0, 0))],
        out_specs=pl.BlockSpec((1, tq, vw), lambda bi, i: (bi, i, 0)),
        out_shape=jax.ShapeDtypeStruct((b, t_q, vw), F32),
        compiler_params=_cparams(("arbitrary", "arbitrary")),
        name="mla_attn",
    )(q, k, v)


def _cache_kv_kernel(ckv_ref, kr_ref, wukv_ref, place_ref, kcat_o, vmla_o, *, heads):
    gw = heads * LANES
    kv = jnp.dot(ckv_ref[...].astype(BF16), wukv_ref[...], preferred_element_type=F32)
    kr_hi = jnp.dot(kr_ref[...].astype(BF16), place_ref[...], preferred_element_type=F32)
    kcat_o[...] = (kv[:, 0:gw] + jnp.concatenate([kr_hi] * heads, axis=1)).astype(BF16)
    vmla_o[...] = kv[:, gw:].astype(BF16)


def _cache_kv(ckv2, kr2, w_ukv, place, heads, tm):
    n, kvr = ckv2.shape
    rdim = kr2.shape[1]
    gw = heads * LANES
    vw = w_ukv.shape[1] - gw
    return pl.pallas_call(
        functools.partial(_cache_kv_kernel, heads=heads),
        grid=(n // tm,),
        in_specs=[pl.BlockSpec((tm, kvr), lambda i: (i, 0)), pl.BlockSpec((tm, rdim), lambda i: (i, 0)),
                  pl.BlockSpec(w_ukv.shape, lambda i: (0, 0)), pl.BlockSpec(place.shape, lambda i: (0, 0))],
        out_specs=[pl.BlockSpec((tm, gw), lambda i: (i, 0)), pl.BlockSpec((tm, vw), lambda i: (i, 0))],
        out_shape=[jax.ShapeDtypeStruct((n, gw), BF16), jax.ShapeDtypeStruct((n, vw), BF16)],
        compiler_params=_cparams(("arbitrary",)),
        name="cache_kv",
    )(ckv2, kr2, w_ukv, place)


def _topk_rows(s, k, payload=None):
    rows = s.shape[0]
    rid = lax.broadcasted_iota(jnp.int32, s.shape, 0).astype(F32)
    vals, picks = [], []
    for _ in range(k):
        m = jnp.max(s, axis=0, keepdims=True)
        first = jnp.min(jnp.where(s == m, rid, float(rows)), axis=0, keepdims=True)
        hit = rid == first
        vals.append(m)
        if payload is None:
            picks.append(first)
        else:
            picks.append(jnp.max(jnp.where(hit, payload, -1.0), axis=0, keepdims=True))
        s = jnp.where(hit, -jnp.inf, s)
    return jnp.concatenate(vals, axis=0), jnp.concatenate(picks, axis=0)


def _staircase(k):
    return [(a, b) for a in range(k) for b in range(k) if (a + 1) * (b + 1) <= k]


def _out_kernel(x_ref, sbo_ref, mlao_ref, gt1_ref, sc2_ref, sh2_ref, gsb_ref, gmla_ref, gpost_ref, gffn_ref,
                wout_ref, wpq_ref, k1_ref, k2_ref, x1_o, h2_o, idx_o, gate_o, q_scr, *, peer_heads, n_keys):
    sb = _rms(sbo_ref[...], gsb_ref[...])
    ml = _rms(mlao_ref[...], gmla_ref[...])
    merged = jnp.concatenate([sb, ml], axis=1).astype(BF16)
    y = jnp.dot(merged, wout_ref[...], preferred_element_type=F32)
    x1 = x_ref[...] + gt1_ref[0] * _rms(y, gpost_ref[...])
    x1_o[...] = x1
    h2 = _rms(x1, gffn_ref[...]) * (1.0 + sc2_ref[0]) + sh2_ref[0]
    h2_o[...] = h2
    q = jnp.dot(h2.astype(BF16), wpq_ref[...], preferred_element_type=F32)
    half = q.shape[1] // (2 * peer_heads)
    for hd in range(peer_heads):
        for side in range(2):
            o = (2 * hd + side) * half
            q_scr[hd, side] = q[:, o:o + half].astype(BF16)
    pairs = _staircase(PEER_TOPK)
    k1 = k1_ref[...]
    k2 = k2_ref[...]

    def per_head(hd, carry):
        s1 = lax.dot_general(k1, q_scr[hd, 0], NT_DIMS, preferred_element_type=F32)
        s2 = lax.dot_general(k2, q_scr[hd, 1], NT_DIMS, preferred_element_type=F32)
        v1, i1 = _topk_rows(s1, PEER_TOPK)
        v2, i2 = _topk_rows(s2, PEER_TOPK)
        fill = (-len(pairs)) % 8
        tokens = v1.shape[1]
        cs = jnp.concatenate([v1[a:a + 1] + v2[b:b + 1] for a, b in pairs]
                             + [jnp.full((fill, tokens), -jnp.inf, F32)], axis=0)
        ci = jnp.concatenate([i1[a:a + 1] * float(n_keys) + i2[b:b + 1] for a, b in pairs]
                             + [jnp.full((fill, tokens), -1.0, F32)], axis=0)
        ts, ti = _topk_rows(cs, PEER_TOPK, payload=ci)
        ex = jnp.exp(ts - ts[0:1])
        gate_o[hd] = ex / jnp.sum(ex, axis=0, keepdims=True)
        idx_o[hd] = ti.astype(jnp.int32)
        return carry

    lax.fori_loop(0, peer_heads, per_head, 0)


def _out_peer(x2, sbo, mlao, gt1, sc2, sh2, wts, dims, tm, tiles_per_batch):
    n, d = x2.shape
    ph, nk = dims["peer_heads"], dims["n_keys"]
    half = wts["w_pq"].shape[1] // (2 * ph)
    full = lambda a: pl.BlockSpec(a.shape, lambda i: (0,) * a.ndim)
    row = lambda w: pl.BlockSpec((tm, w), lambda i: (i, 0))
    ada = lambda a: _ada_spec(a, tm, tiles_per_batch)
    names = ["g_sb", "g_mla", "g_post", "g_ffn", "w_out", "w_pq", "k1", "k2"]
    return pl.pallas_call(
        functools.partial(_out_kernel, peer_heads=ph, n_keys=nk),
        grid=(n // tm,),
        in_specs=[row(d), row(sbo.shape[1]), row(mlao.shape[1]), ada(gt1), ada(sc2), ada(sh2)]
                 + [full(wts[k]) for k in names],
        out_specs=[row(d), row(d),
                   pl.BlockSpec((ph, PEER_TOPK, tm), lambda i: (0, 0, i)),
                   pl.BlockSpec((ph, PEER_TOPK, tm), lambda i: (0, 0, i))],
        out_shape=[jax.ShapeDtypeStruct((n, d), F32), jax.ShapeDtypeStruct((n, d), F32),
                   jax.ShapeDtypeStruct((ph, PEER_TOPK, n), jnp.int32),
                   jax.ShapeDtypeStruct((ph, PEER_TOPK, n), F32)],
        scratch_shapes=[pltpu.VMEM((ph, 2, tm, half), BF16)],
        compiler_params=_cparams(("arbitrary",)),
        name="out_peer",
    )(x2, sbo, mlao, gt1, sc2, sh2, *[wts[k] for k in names])


def _gelu(x):
    return 0.5 * x * (1.0 + lax.erf(x * (2.0 ** -0.5)))


def _split_bf16(x):
    hi = x.astype(BF16)
    return hi, (x - hi.astype(F32)).astype(BF16)


def _peer_kernel(idx_hbm, slot_hbm, tab_hbm, h2_ref, gate_ref, x1_ref, gt2_ref, gpost_ref, y_o,
                 idx_smem, gate_scr, f_scr, idx_sem, row_sem, *bufs,
                 tt, picks, chunks, grp, nsets, ntiles):
    tile = pl.program_id(0)
    par = tile % 2
    ahead = (nsets - 1) * grp
    n_it = tt // (grp * nsets)

    def idx_copy(tl, half):
        return pltpu.make_async_copy(idx_hbm.at[tl], idx_smem.at[half], idx_sem.at[half])

    def row_copy(e, slot, k):
        return pltpu.make_async_copy(tab_hbm.at[e], bufs[slot].at[:, k, :], row_sem.at[slot])

    def issue(half, t, slot):
        for k in range(picks):
            row_copy(idx_smem[half, k, t], slot, k).start(priority=k % 2)

    def wait_slot(slot):
        pltpu.make_async_copy(slot_hbm.at[0], bufs[slot], row_sem.at[slot]).wait()

    @pl.when(tile == 0)
    def _():
        first = idx_copy(0, 0)
        first.start()
        first.wait()
        for t in range(ahead):
            def one(k, carry, t=t):
                row_copy(idx_smem[0, k, t], t, k).start()
                return carry
            lax.fori_loop(0, picks, one, 0)

    nxt = idx_copy((tile + 1) % ntiles, 1 - par)
    nxt.start()
    gate_scr[...] = gate_ref[...].T

    ones8 = jnp.ones((8, LANES), BF16)
    eye = (lax.broadcasted_iota(jnp.int32, (picks, picks), 0)
           == lax.broadcasted_iota(jnp.int32, (picks, picks), 1)).astype(BF16)

    def compute(slot, t):
        buf = bufs[slot]
        u_of = lambda w: pltpu.bitcast(w & jnp.uint32(0xFFFF0000), F32)
        v_of = lambda w: pltpu.bitcast(w << 16, F32)
        xt = h2_ref[pl.ds(t, 1), :]
        part = u_of(buf[0]) * xt[:, 0:LANES]
        for c in range(1, chunks):
            part = part + u_of(buf[c]) * xt[:, c * LANES:(c + 1) * LANES]
        hi, lo = _split_bf16(part)
        s8 = (lax.dot_general(ones8, hi, NT_DIMS, preferred_element_type=F32)
              + lax.dot_general(ones8, lo, NT_DIMS, preferred_element_type=F32))
        a8 = _gelu(s8) * gate_scr[pl.ds(t, 1), :]
        ah, al = _split_bf16(a8)
        reps = picks // 8
        acol = (lax.dot_general(eye, jnp.concatenate([ah] * reps, axis=0), NT_DIMS, preferred_element_type=F32)
                + lax.dot_general(eye, jnp.concatenate([al] * reps, axis=0), NT_DIMS, preferred_element_type=F32))
        outs = [jnp.sum(acol * v_of(buf[c]), axis=0, keepdims=True) for c in range(chunks)]
        f_scr[pl.ds(t, 1), :] = jnp.concatenate(outs, axis=1)

    def body(it, carry):
        @pl.when(it == n_it - 1)
        def _():
            nxt.wait()

        for s in range(nsets):
            t0 = (it * nsets + s) * grp
            for j in range(grp):
                wait_slot(s * grp + j)
            for j in range(grp):
                la = t0 + j + ahead
                cross = la >= tt
                issue(jnp.where(cross, 1 - par, par), jnp.where(cross, la - tt, la),
                      ((s + nsets - 1) % nsets) * grp + j)
                compute(s * grp + j, t0 + j)
        return carry

    lax.fori_loop(0, n_it, body, 0)

    @pl.when(tile == ntiles - 1)
    def _():
        for slot in range(ahead):
            wait_slot(slot)

    y_o[...] = x1_ref[...] + gt2_ref[0] * _rms(f_scr[...], gpost_ref[...])


def _peer_ffn(idx, h2, gate, x1, gt2, g_post, table, tt, tiles_per_batch, grp=4, nsets=4):
    n, d = h2.shape
    picks = gate.shape[0]
    chunks = d // LANES
    ntiles = n // tt
    assert table.shape[1:] == (chunks, LANES) and table.dtype == jnp.uint32
    assert tt % (grp * nsets) == 0 and picks % 8 == 0 and table.shape[0] % picks == 0
    idx_tiles = idx.reshape(picks, ntiles, tt).transpose(1, 0, 2)
    slot_view = table.reshape(-1, chunks, picks, LANES)
    kern = functools.partial(_peer_kernel, tt=tt, picks=picks, chunks=chunks, grp=grp, nsets=nsets,
                             ntiles=ntiles)
    row = lambda w: pl.BlockSpec((tt, w), lambda i: (i, 0))
    hbm = pl.BlockSpec(memory_space=pl.ANY)
    return pl.pallas_call(
        kern,
        grid=(ntiles,),
        in_specs=[hbm, hbm, hbm, row(d), pl.BlockSpec((picks, tt), lambda i: (0, i)), row(d),
                  _ada_spec(gt2, tt, tiles_per_batch), pl.BlockSpec(g_post.shape, lambda i: (0, 0))],
        out_specs=row(d),
        out_shape=jax.ShapeDtypeStruct((n, d), F32),
        scratch_shapes=[pltpu.SMEM((2, picks, tt), jnp.int32),
                        pltpu.VMEM((tt, picks), F32),
                        pltpu.VMEM((tt, d), F32),
                        pltpu.SemaphoreType.DMA((2,)),
                        pltpu.SemaphoreType.DMA((grp * nsets,))]
                       + [pltpu.VMEM((chunks, picks, LANES), jnp.uint32) for _ in range(grp * nsets)],
        compiler_params=_cparams(("arbitrary",)),
        name="peer_ffn",
    )(idx_tiles, slot_view, table, h2, gate, x1, gt2, g_post)


def _rope_table(pos, rdim, q_scale):
    half = rdim // 2
    inv = ROPE_BASE ** (-jnp.arange(half, dtype=F32) / half)
    ang = pos.astype(F32)[:, None] * inv[None, :]
    cos = jnp.concatenate([jnp.cos(ang), jnp.cos(ang)], axis=1)
    sin = jnp.concatenate([-jnp.sin(ang), jnp.sin(ang)], axis=1)
    t = pos.shape[0]
    z = lambda w: jnp.zeros((t, w), F32)
    cq = jnp.concatenate([jnp.ones((t, 64), F32), cos, z(64 - rdim)], axis=1) * q_scale
    sq = jnp.concatenate([z(64), sin, z(64 - rdim)], axis=1) * q_scale
    ck = jnp.concatenate([cos, z(64 - rdim), cos, z(64 - rdim)], axis=1)
    sk = jnp.concatenate([sin, z(64 - rdim), sin, z(64 - rdim)], axis=1)
    return jnp.concatenate([cq, sq, ck, sk], axis=1)


def _pack_experts(u, v):
    bits = lambda a: lax.bitcast_convert_type(a.astype(BF16), jnp.uint16).astype(jnp.uint32)
    e, d = u.shape
    return ((bits(u) << 16) | bits(v)).reshape(e, d // LANES, LANES)


def _swap_halves(w):
    half = w.shape[-1] // 2
    return jnp.concatenate([w[..., half:], w[..., :half]], axis=-1)


def _prep_weights(w_in, g_q_lat, g_kv_lat, w_uq, w_uk, w_uv, dims):
    d = w_in.shape[0]
    sbw, qr, kvr, rdim = dims["sbw"], dims["qr"], dims["kvr"], dims["rdim"]
    heads, nope, vdim = dims["mla_heads"], dims["nope"], dims["vdim"]
    assert nope == 64 and rdim <= 64 and vdim == 64 and dims["sb_dim"] == 64
    o = 3 * sbw
    w_kr = w_in[:, o + qr + kvr:]
    zk = jnp.zeros((d, 64 - rdim), F32)
    kr_a = jnp.concatenate([w_kr, zk, w_kr, zk], axis=1)
    kr_b = jnp.concatenate([_swap_halves(w_kr), zk, _swap_halves(w_kr), zk], axis=1)
    w_all = jnp.concatenate([w_in[:, :o + qr + kvr], kr_a, kr_b], axis=1).astype(BF16)
    uq = w_uq.reshape(qr, heads, nope + rdim)
    zq = jnp.zeros((qr, heads, LANES - nope - rdim), F32)
    uq_a = jnp.concatenate([uq, zq], axis=2)
    uq_b = jnp.concatenate([jnp.zeros((qr, heads, nope), F32), _swap_halves(uq[..., nope:]), zq], axis=2)
    w_uq2 = jnp.concatenate([uq_a.reshape(qr, -1), uq_b.reshape(qr, -1)], axis=1).astype(BF16)
    uk = jnp.concatenate([w_uk, jnp.zeros((kvr, heads, LANES - nope), F32)], axis=2).reshape(kvr, -1)
    w_ukv = jnp.concatenate([uk, w_uv.reshape(kvr, -1)], axis=1).astype(BF16)
    place = (jnp.arange(rdim)[:, None] + 64 == jnp.arange(LANES)[None, :]).astype(BF16)
    return dict(w_all=w_all, w_uq2=w_uq2, w_ukv=w_ukv, place=place,
                g_q=g_q_lat.reshape(1, -1), g_kv=g_kv_lat.reshape(1, -1))


def _layer(x, ada, pos0, past, lw, dims, tq_sb, tq_mla, tk_sb, tk_mla, tm_in, tm_out, tt):
    b, t, d = x.shape
    n = b * t
    x2 = x.reshape(n, d)
    per_token = t < tm_in
    if per_token:
        sh1, sc1, gt1, sh2, sc2, gt2 = [jnp.repeat(a, t, axis=0)[None] for a in ada]
        tiles_in = tiles_out = tiles_tt = 1
    else:
        sh1, sc1, gt1, sh2, sc2, gt2 = [a[:, None, :] for a in ada]
        tiles_in, tiles_out, tiles_tt = t // tm_in, t // tm_out, t // tt
    tm_in, tm_out, tt = min(tm_in, n), min(tm_out, n), min(tt, n)
    heads = dims["mla_heads"]
    q_scale = (dims["nope"] + dims["rdim"]) ** -0.5
    rope_tab = _rope_table(pos0 + jnp.arange(t), dims["rdim"], q_scale)
    rope_tab = jnp.tile(rope_tab, (b, 1)) if per_token else rope_tab
    (sbq, sbk, sbv, sbkb, sbvb, ckvn, kr, qcat, kcat, vmla) = _in_proj(
        x2, sc1, sh1, lw["g_mix_pre"], rope_tab, lw, dims, tm_in, tiles_in)
    sbw, gw, vw = dims["sbw"], heads * LANES, vmla.shape[1]
    r3 = lambda a: a.reshape(b, t, a.shape[-1])
    if past is None:
        q_off, t_valid = 0, t
        k_sb, v_sb, k_m, v_m = r3(sbkb), r3(sbvb), r3(kcat), r3(vmla)
        q_sb, q_m = r3(sbq), r3(qcat)
        tq_s, tq_m = tq_sb, tq_mla
    else:
        c_k, c_v, c_ckv, c_kr = past
        p_len = c_k.shape[1]
        q_off, t_valid = p_len, p_len + t
        tq_s = tq_m = t
        kc, vc = _cache_kv(c_ckv.reshape(b * p_len, -1), c_kr.reshape(b * p_len, -1), lw["w_ukv"],
                           lw["place"], heads, tm_in)

        def with_cache(cache, new, tk):
            pad = (-(p_len + t)) % tk
            return jnp.concatenate([cache, new, jnp.zeros((b, pad, new.shape[-1]), new.dtype)], axis=1)

        k_sb = with_cache(c_k.reshape(b, p_len, sbw).astype(BF16), r3(sbkb), tk_sb)
        v_sb = with_cache(c_v.reshape(b, p_len, sbw).astype(BF16), r3(sbvb), tk_sb)
        k_m = with_cache(kc.reshape(b, p_len, gw), r3(kcat), tk_mla)
        v_m = with_cache(vc.reshape(b, p_len, vw), r3(vmla), tk_mla)
        q_sb, q_m = r3(sbq), r3(qcat)
    sbo = _sb_attn(q_sb, k_sb, v_sb, tq_s, tk_sb, q_off)
    mlao = _mla_attn(q_m, k_m, v_m, tq_m, tk_mla, q_off, t_valid)
    x1, h2, idx, gate = _out_peer(x2, sbo.reshape(n, sbw), mlao.reshape(n, vw), gt1, sc2, sh2, lw, dims,
                                  tm_out, tiles_out)
    picks = dims["peer_heads"] * PEER_TOPK
    y = _peer_ffn(idx.reshape(picks, n), h2, gate.reshape(picks, n), x1, gt2, lw["g_post_ffn"], lw["table"],
                  tt, tiles_tt)
    new = (sbk.reshape(b, t, dims["sb_heads"], dims["sb_dim"]), sbv.reshape(b, t, dims["sb_heads"], dims["sb_dim"]),
           ckvn.reshape(b, t, -1), kr.reshape(b, t, -1))
    return y.reshape(b, t, d), new


def kernel(x_prompt, x_sample, c_prompt, c_sample, cache_sb_k, cache_sb_v, cache_mla_ckv, cache_mla_krope,
           w_ada, b_ada, g_mix_pre, g_mix_post, g_ffn_pre, g_ffn_post, w_in, g_q_lat, g_kv_lat, w_uq, w_uk,
           w_uv, g_sb_out, g_mla_out, w_out, w_peer_q, sub_keys_1, sub_keys_2, u_experts, v_experts):
    depth = w_ada.shape[0]
    d = x_prompt.shape[-1]
    sb_heads, sb_dim = cache_sb_k.shape[-2:]
    kvr, mla_heads, nope = w_uk.shape[1:]
    dims = dict(sb_heads=sb_heads, sb_dim=sb_dim, sbw=sb_heads * sb_dim, qr=w_uq.shape[1], kvr=kvr,
                rdim=cache_mla_krope.shape[-1], mla_heads=mla_heads, nope=nope, vdim=w_uv.shape[-1],
                n_keys=sub_keys_1.shape[1], peer_heads=w_peer_q.shape[2] // (2 * sub_keys_1.shape[2]))
    bp, bs = c_prompt.shape[0], c_sample.shape[0]
    pad = (-(bp + bs)) % 8
    y_p, y_s = x_prompt, x_sample
    outs_p, outs_s = [], []
    for l in range(depth):
        c_all = jnp.concatenate([c_prompt, c_sample, jnp.zeros((pad, d), F32)], axis=0)
        ada = _ada(c_all, w_ada[l], b_ada[l])
        ada_p = jnp.split(ada[:bp], 6, axis=1)
        ada_s = jnp.split(ada[bp:bp + bs], 6, axis=1)
        lw = _prep_weights(w_in[l], g_q_lat[l], g_kv_lat[l], w_uq[l], w_uk[l], w_uv[l], dims)
        e = u_experts.shape[1]
        lw.update(
            g_mix_pre=g_mix_pre[l].reshape(1, d), g_post=g_mix_post[l].reshape(1, d),
            g_ffn=g_ffn_pre[l].reshape(1, d), g_post_ffn=g_ffn_post[l].reshape(1, d),
            g_sb=g_sb_out[l].reshape(1, -1), g_mla=g_mla_out[l].reshape(1, -1),
            w_out=w_out[l].astype(BF16), w_pq=w_peer_q[l].astype(BF16),
            k1=sub_keys_1[l].astype(BF16), k2=sub_keys_2[l].astype(BF16),
            table=_pack_experts(u_experts[l], v_experts[l]))
        tiles = dict(tq_sb=128, tq_mla=512, tk_sb=128, tk_mla=512, tm_in=512, tm_out=256, tt=256)
        y_p, new_p = _layer(y_p, ada_p, 0, None, lw, dims, **tiles)
        past = (cache_sb_k[l], cache_sb_v[l], cache_mla_ckv[l], cache_mla_krope[l])
        y_s, new_s = _layer(y_s, ada_s, cache_sb_k.shape[2], past, lw, dims, **tiles)
        outs_p.append(new_p)
        outs_s.append(new_s)
    stack = lambda outs, k: jnp.stack([o[k] for o in outs])
    return (y_p, y_s, stack(outs_p, 0), stack(outs_p, 1), stack(outs_p, 2), stack(outs_p, 3),
            stack(outs_s, 0), stack(outs_s, 1), stack(outs_s, 2), stack(outs_s, 3))
```

```python
import functools

import jax
import jax.numpy as jnp
import numpy as np
from jax import lax
from jax.experimental import pallas as pl
from jax.experimental.pallas import tpu as pltpu

F32 = jnp.float32
BF16 = jnp.bfloat16

CHUNK = 64
ROPE_BASE = 10000.0
NORM_EPS = 1e-6
PEER_TOPK = 16
LANES = 128
VMEM_LIMIT = 56 * 1024 * 1024
SB_LOG_FLOOR = -88.0
NT_DIMS = (((1,), (1,)), ((), ()))


def _rms(x, g):
    return x * lax.rsqrt(jnp.mean(x * x, axis=-1, keepdims=True) + NORM_EPS) * g


def _cparams(sem):
    return pltpu.CompilerParams(dimension_semantics=sem, vmem_limit_bytes=VMEM_LIMIT)


def _ada_kernel(c_ref, w_ref, b_ref, o_ref):
    c = c_ref[...]
    s = c * jax.nn.sigmoid(c)
    o_ref[...] = jnp.dot(s.astype(BF16), w_ref[...].astype(BF16), preferred_element_type=F32) + b_ref[...]


def _ada(c_all, w_ada, b_ada):
    rows, d = c_all.shape
    width = w_ada.shape[1]
    return pl.pallas_call(
        _ada_kernel,
        grid=(width // d,),
        in_specs=[pl.BlockSpec((rows, d), lambda j: (0, 0)),
                  pl.BlockSpec((d, d), lambda j: (0, j)),
                  pl.BlockSpec((1, d), lambda j: (0, j))],
        out_specs=pl.BlockSpec((rows, d), lambda j: (0, j)),
        out_shape=jax.ShapeDtypeStruct((rows, width), F32),
        compiler_params=_cparams(("arbitrary",)),
        name="ada",
    )(c_all, w_ada, b_ada.reshape(1, width))


def _in_kernel(x_ref, sc_ref, sh_ref, g_ref, rope_ref, wall_ref, gq_ref, gkv_ref, wuq_ref, wukv_ref,
               sbq_o, sbk_o, sbv_o, sbkb_o, sbvb_o, ckv_o, kr_o, qcat_o, kcat_o, vmla_o,
               *, sbw, qr, kvr, rdim, heads, sb_scale):
    x = x_ref[...]
    h = _rms(x, g_ref[...]) * (1.0 + sc_ref[0]) + sh_ref[0]
    proj = jnp.dot(h.astype(BF16), wall_ref[...], preferred_element_type=F32)
    o = 0
    sbq_o[...] = (proj[:, o:o + sbw] * sb_scale).astype(BF16); o += sbw
    k = proj[:, o:o + sbw]; o += sbw
    v = proj[:, o:o + sbw]; o += sbw
    sbk_o[...] = k
    sbv_o[...] = v
    sbkb_o[...] = k.astype(BF16)
    sbvb_o[...] = v.astype(BF16)
    qlat = proj[:, o:o + qr]; o += qr
    ckv = proj[:, o:o + kvr]; o += kvr
    kr_a = proj[:, o:o + LANES]; o += LANES
    kr_b = proj[:, o:o + LANES]
    rt = rope_ref[...]
    cq, sq = rt[:, 0:LANES], rt[:, LANES:2 * LANES]
    ck, sk = rt[:, 2 * LANES:3 * LANES], rt[:, 3 * LANES:4 * LANES]
    kr128 = kr_a * ck + kr_b * sk
    kr_o[...] = kr128[:, 0:rdim]
    qn = _rms(qlat, gq_ref[...]).astype(BF16)
    q2 = jnp.dot(qn, wuq_ref[...], preferred_element_type=F32)
    gw = heads * LANES
    cq8 = jnp.concatenate([cq] * heads, axis=1)
    sq8 = jnp.concatenate([sq] * heads, axis=1)
    qcat_o[...] = (q2[:, 0:gw] * cq8 + q2[:, gw:2 * gw] * sq8).astype(BF16)
    ckvn = _rms(ckv, gkv_ref[...])
    ckv_o[...] = ckvn
    kv = jnp.dot(ckvn.astype(BF16), wukv_ref[...], preferred_element_type=F32)
    lane = lax.broadcasted_iota(jnp.int32, kr128.shape, 1)
    kr_hi = jnp.where(lane >= 64, kr128, 0.0)
    kcat_o[...] = (kv[:, 0:gw] + jnp.concatenate([kr_hi] * heads, axis=1)).astype(BF16)
    vmla_o[...] = kv[:, gw:].astype(BF16)


def _ada_spec(arr, tm, tiles_per_batch):
    d = arr.shape[-1]
    if arr.shape[1] == 1:
        return pl.BlockSpec((1, 1, d), lambda i: (i // tiles_per_batch, 0, 0))
    return pl.BlockSpec((1, tm, d), lambda i: (0, i, 0))


def _in_proj(x2, sc, sh, g_pre, rope_tab, wts, dims, tm, tiles_per_batch):
    n, d = x2.shape
    sbw, qr, kvr, rdim, heads = dims["sbw"], dims["qr"], dims["kvr"], dims["rdim"], dims["mla_heads"]
    gw = heads * LANES
    vw = wts["w_ukv"].shape[1] - gw
    full = lambda a: pl.BlockSpec(a.shape, lambda i: (0,) * a.ndim)
    row = lambda w: pl.BlockSpec((tm, w), lambda i: (i, 0))
    kern = functools.partial(_in_kernel, sbw=sbw, qr=qr, kvr=kvr, rdim=rdim, heads=heads,
                             sb_scale=dims["sb_dim"] ** -0.5)
    outs = [(sbw, BF16), (sbw, F32), (sbw, F32), (sbw, BF16), (sbw, BF16), (kvr, F32), (rdim, F32),
            (gw, BF16), (gw, BF16), (vw, BF16)]
    return pl.pallas_call(
        kern,
        grid=(n // tm,),
        in_specs=[row(d), _ada_spec(sc, tm, tiles_per_batch), _ada_spec(sh, tm, tiles_per_batch),
                  full(g_pre), pl.BlockSpec((tm, 4 * LANES), lambda i: (i % (rope_tab.shape[0] // tm), 0)),
                  full(wts["w_all"]), full(wts["g_q"]), full(wts["g_kv"]),
                  full(wts["w_uq2"]), full(wts["w_ukv"])],
        out_specs=[row(w) for w, _ in outs],
        out_shape=[jax.ShapeDtypeStruct((n, w), dt) for w, dt in outs],
        compiler_params=_cparams(("arbitrary",)),
        name="in_proj",
    )(x2, sc, sh, g_pre, rope_tab, wts["w_all"], wts["g_q"], wts["g_kv"], wts["w_uq2"], wts["w_ukv"])


def _sb_block(qm, kb, vb, c, acc, mask, tri):
    z = lax.dot_general(qm, kb, NT_DIMS, preferred_element_type=F32)
    sp = jnp.maximum(z, 0.0) + jnp.log(1.0 + jnp.exp(-jnp.abs(z)))
    lk = -sp if mask is None else jnp.where(mask, -sp, 0.0)
    hi = lk.astype(BF16)
    lo = (lk - hi.astype(F32)).astype(BF16)
    incl = jnp.dot(hi, tri, preferred_element_type=F32) + jnp.dot(lo, tri, preferred_element_type=F32)
    a = jnp.exp(z + incl + c)
    if mask is not None:
        a = jnp.where(mask, a, 0.0)
    acc = acc + jnp.dot(a.astype(BF16), vb, preferred_element_type=F32)
    return c + incl[:, 0:1], acc


def _sb_kernel(q_ref, k_ref, v_ref, o_ref, qm_scr, c_scr, acc_scr, *, tq, tk, q_off, pairs):
    i = pl.program_id(1)
    q_start = q_off + i * tq
    jm = q_start // tk
    r = lax.broadcasted_iota(jnp.int32, (tk, tk), 0)
    cc = lax.broadcasted_iota(jnp.int32, (tk, tk), 1)
    tri = (r >= cc).astype(BF16)
    qpos = q_start + lax.broadcasted_iota(jnp.int32, (tq, tk), 0)
    kpos = jm * tk + lax.broadcasted_iota(jnp.int32, (tq, tk), 1)
    causal = jnp.concatenate([kpos < qpos] * 2, axis=0)
    lane = lax.broadcasted_iota(jnp.int32, (tq, LANES), 1)
    for p in range(pairs):
        qp = q_ref[0, :, p * LANES:(p + 1) * LANES]
        zero = jnp.zeros_like(qp)
        qm_scr[p] = jnp.concatenate([jnp.where(lane < 64, qp, zero), jnp.where(lane >= 64, qp, zero)], axis=0)
    c_scr[...] = jnp.zeros_like(c_scr)
    acc_scr[...] = jnp.zeros_like(acc_scr)

    def block(j, mask):
        ks = pl.multiple_of(j * tk, tk)
        worst = None
        for p in range(pairs):
            sl = slice(p * LANES, (p + 1) * LANES)
            c, acc = _sb_block(qm_scr[p], k_ref[0, pl.ds(ks, tk), sl], v_ref[0, pl.ds(ks, tk), sl],
                               c_scr[p], acc_scr[p], mask, tri)
            c_scr[p] = c
            acc_scr[p] = acc
            worst = c if worst is None else jnp.maximum(worst, c)
        return jnp.max(worst) > SB_LOG_FLOOR

    def body(s):
        j, _ = s
        alive = block(j, None)
        return j - 1, jnp.logical_and(j > 0, alive).astype(jnp.int32)

    go0 = jnp.logical_and(jm > 0, block(jm, causal)).astype(jnp.int32)
    lax.while_loop(lambda s: s[1] > 0, body, (jm - 1, go0))
    outs = [jnp.where(lane < 64, acc_scr[p, 0:tq], acc_scr[p, tq:2 * tq]) for p in range(pairs)]
    o_ref[0] = jnp.concatenate(outs, axis=1)


def _sb_attn(q, k, v, tq, tk, q_off):
    b, t_q, w = q.shape
    t_k = k.shape[1]
    assert tk % tq == 0 and q_off % tk == 0 and q_off + t_q <= t_k
    kern = functools.partial(_sb_kernel, tq=tq, tk=tk, q_off=q_off, pairs=w // LANES)
    return pl.pallas_call(
        kern,
        grid=(b, t_q // tq),
        in_specs=[pl.BlockSpec((1, tq, w), lambda bi, i: (bi, i, 0)),
                  pl.BlockSpec((1, t_k, w), lambda bi, i: (bi, 0, 0)),
                  pl.BlockSpec((1, t_k, w), lambda bi, i: (bi, 0, 0))],
        out_specs=pl.BlockSpec((1, tq, w), lambda bi, i: (bi, i, 0)),
        out_shape=jax.ShapeDtypeStruct((b, t_q, w), F32),
        scratch_shapes=[pltpu.VMEM((w // LANES, 2 * tq, LANES), BF16),
                        pltpu.VMEM((w // LANES, 2 * tq, 1), F32),
                        pltpu.VMEM((w // LANES, 2 * tq, LANES), F32)],
        compiler_params=_cparams(("arbitrary", "arbitrary")),
        name="sb_attn",
    )(q, k, v)


def _mla_kernel(q_ref, k_ref, v_ref, o_ref, *, tq, tk, q_off, t_valid, heads):
    i = pl.program_id(1)
    q_start = q_off + i * tq
    jm = q_start // tk
    qpos = q_start + lax.broadcasted_iota(jnp.int32, (tq, tk), 0)
    kpos = jm * tk + lax.broadcasted_iota(jnp.int32, (tq, tk), 1)
    shift = CHUNK.bit_length() - 1
    visible = jnp.logical_and(jnp.right_shift(kpos, shift) <= jnp.right_shift(qpos, shift), kpos < t_valid)
    lane = lax.broadcasted_iota(jnp.int32, (tq, LANES), 1)
    outs = []
    for h in range(heads):
        sl = slice(h * LANES, (h + 1) * LANES)
        vsl = slice((h // 2) * LANES, (h // 2 + 1) * LANES)
        qh = q_ref[0, :, sl]

        def blk(j, m, l, acc, mask):
            ks = pl.multiple_of(j * tk, tk)
            s = lax.dot_general(qh, k_ref[0, pl.ds(ks, tk), sl], NT_DIMS, preferred_element_type=F32)
            if mask is not None:
                s = jnp.where(mask, s, -jnp.inf)
            m_new = jnp.maximum(m, jnp.max(s, axis=1, keepdims=True))
            alpha = jnp.exp(m - m_new)
            p = jnp.exp(s - m_new)
            l = alpha * l + jnp.sum(p, axis=1, keepdims=True)
            acc = alpha * acc + jnp.dot(p.astype(BF16), v_ref[0, pl.ds(ks, tk), vsl],
                                        preferred_element_type=F32)
            return m_new, l, acc

        m, l, acc = blk(jm, jnp.full((tq, 1), -jnp.inf, F32), jnp.zeros((tq, 1), F32),
                        jnp.zeros((tq, LANES), F32), visible)
        m, l, acc = lax.fori_loop(0, jm, lambda j, s: blk(j, *s, None), (m, l, acc))
        outs.append(acc / l)
    merged = [jnp.where(lane < 64, outs[2 * p], outs[2 * p + 1]) for p in range(heads // 2)]
    o_ref[0] = jnp.concatenate(merged, axis=1)


def _mla_attn(q, k, v, tq, tk, q_off, t_valid):
    b, t_q, gw = q.shape
    t_k = k.shape[1]
    vw = v.shape[2]
    assert tk % tq == 0 and q_off % tk == 0 and q_off + t_q <= t_k and tk % CHUNK == 0
    kern = functools.partial(_mla_kernel, tq=tq, tk=tk, q_off=q_off, t_valid=t_valid, heads=gw // LANES)
    return pl.pallas_call(
        kern,
        grid=(b, t_q // tq),
        in_specs=[pl.BlockSpec((1, tq, gw), lambda bi, i: (bi, i, 0)),
                  pl.BlockSpec((1, t_k, gw), lambda bi, i: (bi, 0, 0)),
                  pl.BlockSpec((1, t_k, vw), lambda bi, i: (bi, 0, 0))],
        out_specs=pl.BlockSpec((1, tq, vw), lambda bi, i: (bi, i, 0)),
        out_shape=jax.ShapeDtypeStruct((b, t_q, vw), F32),
        compiler_params=_cparams(("arbitrary", "arbitrary")),
        name="mla_attn",
    )(q, k, v)


def _cache_kv_kernel(ckv_ref, kr_ref, wukv_ref, place_ref, kcat_o, vmla_o, *, heads):
    gw = heads * LANES
    kv = jnp.dot(ckv_ref[...].astype(BF16), wukv_ref[...], preferred_element_type=F32)
    kr_hi = jnp.dot(kr_ref[...].astype(BF16), place_ref[...], preferred_element_type=F32)
    kcat_o[...] = (kv[:, 0:gw] + jnp.concatenate([kr_hi] * heads, axis=1)).astype(BF16)
    vmla_o[...] = kv[:, gw:].astype(BF16)


def _cache_kv(ckv2, kr2, w_ukv, place, heads, tm):
    n, kvr = ckv2.shape
    rdim = kr2.shape[1]
    gw = heads * LANES
    vw = w_ukv.shape[1] - gw
    return pl.pallas_call(
        functools.partial(_cache_kv_kernel, heads=heads),
        grid=(n // tm,),
        in_specs=[pl.BlockSpec((tm, kvr), lambda i: (i, 0)), pl.BlockSpec((tm, rdim), lambda i: (i, 0)),
                  pl.BlockSpec(w_ukv.shape, lambda i: (0, 0)), pl.BlockSpec(place.shape, lambda i: (0, 0))],
        out_specs=[pl.BlockSpec((tm, gw), lambda i: (i, 0)), pl.BlockSpec((tm, vw), lambda i: (i, 0))],
        out_shape=[jax.ShapeDtypeStruct((n, gw), BF16), jax.ShapeDtypeStruct((n, vw), BF16)],
        compiler_params=_cparams(("arbitrary",)),
        name="cache_kv",
    )(ckv2, kr2, w_ukv, place)


def _topk_rows(s, k, payload=None):
    rows = s.shape[0]
    rid = lax.broadcasted_iota(jnp.int32, s.shape, 0).astype(F32)
    vals, picks = [], []
    for _ in range(k):
        m = jnp.max(s, axis=0, keepdims=True)
        first = jnp.min(jnp.where(s == m, rid, float(rows)), axis=0, keepdims=True)
        hit = rid == first
        vals.append(m)
        if payload is None:
            picks.append(first)
        else:
            picks.append(jnp.max(jnp.where(hit, payload, -1.0), axis=0, keepdims=True))
        s = jnp.where(hit, -jnp.inf, s)
    return jnp.concatenate(vals, axis=0), jnp.concatenate(picks, axis=0)


def _staircase(k):
    return [(a, b) for a in range(k) for b in range(k) if (a + 1) * (b + 1) <= k]


def _out_kernel(x_ref, sbo_ref, mlao_ref, gt1_ref, sc2_ref, sh2_ref, gsb_ref, gmla_ref, gpost_ref, gffn_ref,
                wout_ref, wpq_ref, k1_ref, k2_ref, x1_o, h2_o, idx_o, gate_o, q_scr, *, peer_heads, n_keys):
    sb = _rms(sbo_ref[...], gsb_ref[...])
    ml = _rms(mlao_ref[...], gmla_ref[...])
    merged = jnp.concatenate([sb, ml], axis=1).astype(BF16)
    y = jnp.dot(merged, wout_ref[...], preferred_element_type=F32)
    x1 = x_ref[...] + gt1_ref[0] * _rms(y, gpost_ref[...])
    x1_o[...] = x1
    h2 = _rms(x1, gffn_ref[...]) * (1.0 + sc2_ref[0]) + sh2_ref[0]
    h2_o[...] = h2
    q = jnp.dot(h2.astype(BF16), wpq_ref[...], preferred_element_type=F32)
    half = q.shape[1] // (2 * peer_heads)
    for hd in range(peer_heads):
        for side in range(2):
            o = (2 * hd + side) * half
            q_scr[hd, side] = q[:, o:o + half].astype(BF16)
    pairs = _staircase(PEER_TOPK)
    k1 = k1_ref[...]
    k2 = k2_ref[...]

    def per_head(hd, carry):
        s1 = lax.dot_general(k1, q_scr[hd, 0], NT_DIMS, preferred_element_type=F32)
        s2 = lax.dot_general(k2, q_scr[hd, 1], NT_DIMS, preferred_element_type=F32)
        v1, i1 = _topk_rows(s1, PEER_TOPK)
        v2, i2 = _topk_rows(s2, PEER_TOPK)
        fill = (-len(pairs)) % 8
        tokens = v1.shape[1]
        cs = jnp.concatenate([v1[a:a + 1] + v2[b:b + 1] for a, b in pairs]
                             + [jnp.full((fill, tokens), -jnp.inf, F32)], axis=0)
        ci = jnp.concatenate([i1[a:a + 1] * float(n_keys) + i2[b:b + 1] for a, b in pairs]
                             + [jnp.full((fill, tokens), -1.0, F32)], axis=0)
        ts, ti = _topk_rows(cs, PEER_TOPK, payload=ci)
        ex = jnp.exp(ts - ts[0:1])
        gate_o[hd] = ex / jnp.sum(ex, axis=0, keepdims=True)
        idx_o[hd] = ti.astype(jnp.int32)
        return carry

    lax.fori_loop(0, peer_heads, per_head, 0)


def _out_peer(x2, sbo, mlao, gt1, sc2, sh2, wts, dims, tm, tiles_per_batch):
    n, d = x2.shape
    ph, nk = dims["peer_heads"], dims["n_keys"]
    half = wts["w_pq"].shape[1] // (2 * ph)
    full = lambda a: pl.BlockSpec(a.shape, lambda i: (0,) * a.ndim)
    row = lambda w: pl.BlockSpec((tm, w), lambda i: (i, 0))
    ada = lambda a: _ada_spec(a, tm, tiles_per_batch)
    names = ["g_sb", "g_mla", "g_post", "g_ffn", "w_out", "w_pq", "k1", "k2"]
    return pl.pallas_call(
        functools.partial(_out_kernel, peer_heads=ph, n_keys=nk),
        grid=(n // tm,),
        in_specs=[row(d), row(sbo.shape[1]), row(mlao.shape[1]), ada(gt1), ada(sc2), ada(sh2)]
                 + [full(wts[k]) for k in names],
        out_specs=[row(d), row(d),
                   pl.BlockSpec((ph, PEER_TOPK, tm), lambda i: (0, 0, i)),
                   pl.BlockSpec((ph, PEER_TOPK, tm), lambda i: (0, 0, i))],
        out_shape=[jax.ShapeDtypeStruct((n, d), F32), jax.ShapeDtypeStruct((n, d), F32),
                   jax.ShapeDtypeStruct((ph, PEER_TOPK, n), jnp.int32),
                   jax.ShapeDtypeStruct((ph, PEER_TOPK, n), F32)],
        scratch_shapes=[pltpu.VMEM((ph, 2, tm, half), BF16)],
        compiler_params=_cparams(("arbitrary",)),
        name="out_peer",
    )(x2, sbo, mlao, gt1, sc2, sh2, *[wts[k] for k in names])


def _gelu(x):
    return 0.5 * x * (1.0 + lax.erf(x * (2.0 ** -0.5)))


def _split_bf16(x):
    hi = x.astype(BF16)
    return hi, (x - hi.astype(F32)).astype(BF16)


def _peer_kernel(idx_hbm, slot_hbm, tab_hbm, h2_ref, gate_ref, x1_ref, gt2_ref, gpost_ref, y_o,
                 idx_smem, gate_scr, f_scr, idx_sem, row_sem, *bufs,
                 tt, picks, chunks, grp, nsets, ntiles):
    tile = pl.program_id(0)
    par = tile % 2
    ahead = (nsets - 1) * grp
    n_it = tt // (grp * nsets)

    def idx_copy(tl, half):
        return pltpu.make_async_copy(idx_hbm.at[tl], idx_smem.at[half], idx_sem.at[half])

    def row_copy(e, slot, k):
        return pltpu.make_async_copy(tab_hbm.at[e], bufs[slot].at[:, k, :], row_sem.at[slot])

    def issue(half, t, slot):
        for k in range(picks):
            row_copy(idx_smem[half, k, t], slot, k).start(priority=k % 2)

    def wait_slot(slot):
        pltpu.make_async_copy(slot_hbm.at[0], bufs[slot], row_sem.at[slot]).wait()

    @pl.when(tile == 0)
    def _():
        first = idx_copy(0, 0)
        first.start()
        first.wait()
        for t in range(ahead):
            def one(k, carry, t=t):
                row_copy(idx_smem[0, k, t], t, k).start()
                return carry
            lax.fori_loop(0, picks, one, 0)

    nxt = idx_copy((tile + 1) % ntiles, 1 - par)
    nxt.start()
    gate_scr[...] = gate_ref[...].T

    ones8 = jnp.ones((8, LANES), BF16)
    eye = (lax.broadcasted_iota(jnp.int32, (picks, picks), 0)
           == lax.broadcasted_iota(jnp.int32, (picks, picks), 1)).astype(BF16)

    def compute(slot, t):
        buf = bufs[slot]
        xt = h2_ref[pl.ds(t, 1), :]
        part = buf[0] * xt[:, 0:LANES]
        for c in range(1, chunks):
            part = part + buf[c] * xt[:, c * LANES:(c + 1) * LANES]
        hi, lo = _split_bf16(part)
        s8 = (lax.dot_general(ones8, hi, NT_DIMS, preferred_element_type=F32)
              + lax.dot_general(ones8, lo, NT_DIMS, preferred_element_type=F32))
        a8 = _gelu(s8) * gate_scr[pl.ds(t, 1), :]
        ah, al = _split_bf16(a8)
        reps = picks // 8
        acol = (lax.dot_general(eye, jnp.concatenate([ah] * reps, axis=0), NT_DIMS, preferred_element_type=F32)
                + lax.dot_general(eye, jnp.concatenate([al] * reps, axis=0), NT_DIMS, preferred_element_type=F32))
        outs = [jnp.sum(acol * buf[chunks + c], axis=0, keepdims=True) for c in range(chunks)]
        f_scr[pl.ds(t, 1), :] = jnp.concatenate(outs, axis=1)

    def body(it, carry):
        @pl.when(it == n_it - 1)
        def _():
            nxt.wait()

        for s in range(nsets):
            t0 = (it * nsets + s) * grp
            for j in range(grp):
                wait_slot(s * grp + j)
            for j in range(grp):
                la = t0 + j + ahead
                cross = la >= tt
                issue(jnp.where(cross, 1 - par, par), jnp.where(cross, la - tt, la),
                      ((s + nsets - 1) % nsets) * grp + j)
                compute(s * grp + j, t0 + j)
        return carry

    lax.fori_loop(0, n_it, body, 0)

    @pl.when(tile == ntiles - 1)
    def _():
        for slot in range(ahead):
            wait_slot(slot)

    y_o[...] = x1_ref[...] + gt2_ref[0] * _rms(f_scr[...], gpost_ref[...])


def _peer_ffn(idx, h2, gate, x1, gt2, g_post, table, tt, tiles_per_batch, grp=4, nsets=4):
    n, d = h2.shape
    picks = gate.shape[0]
    chunks = d // LANES
    ntiles = n // tt
    assert table.shape[1:] == (2 * chunks, LANES) and tt % (grp * nsets) == 0 and picks % 8 == 0
    assert table.shape[0] % picks == 0
    idx_tiles = idx.reshape(picks, ntiles, tt).transpose(1, 0, 2)
    slot_view = table.reshape(-1, 2 * chunks, picks, LANES)
    kern = functools.partial(_peer_kernel, tt=tt, picks=picks, chunks=chunks, grp=grp, nsets=nsets,
                             ntiles=ntiles)
    row = lambda w: pl.BlockSpec((tt, w), lambda i: (i, 0))
    hbm = pl.BlockSpec(memory_space=pl.ANY)
    return pl.pallas_call(
        kern,
        grid=(ntiles,),
        in_specs=[hbm, hbm, hbm, row(d), pl.BlockSpec((picks, tt), lambda i: (0, i)), row(d),
                  _ada_spec(gt2, tt, tiles_per_batch), pl.BlockSpec(g_post.shape, lambda i: (0, 0))],
        out_specs=row(d),
        out_shape=jax.ShapeDtypeStruct((n, d), F32),
        scratch_shapes=[pltpu.SMEM((2, picks, tt), jnp.int32),
                        pltpu.VMEM((tt, picks), F32),
                        pltpu.VMEM((tt, d), F32),
                        pltpu.SemaphoreType.DMA((2,)),
                        pltpu.SemaphoreType.DMA((grp * nsets,))]
                       + [pltpu.VMEM((2 * chunks, picks, LANES), F32) for _ in range(grp * nsets)],
        compiler_params=_cparams(("arbitrary",)),
        name="peer_ffn",
    )(idx_tiles, slot_view, table, h2, gate, x1, gt2, g_post)


def _rope_table(pos, rdim, q_scale):
    half = rdim // 2
    inv = ROPE_BASE ** (-jnp.arange(half, dtype=F32) / half)
    ang = pos.astype(F32)[:, None] * inv[None, :]
    cos = jnp.concatenate([jnp.cos(ang), jnp.cos(ang)], axis=1)
    sin = jnp.concatenate([-jnp.sin(ang), jnp.sin(ang)], axis=1)
    t = pos.shape[0]
    z = lambda w: jnp.zeros((t, w), F32)
    cq = jnp.concatenate([jnp.ones((t, 64), F32), cos, z(64 - rdim)], axis=1) * q_scale
    sq = jnp.concatenate([z(64), sin, z(64 - rdim)], axis=1) * q_scale
    ck = jnp.concatenate([cos, z(64 - rdim), cos, z(64 - rdim)], axis=1)
    sk = jnp.concatenate([sin, z(64 - rdim), sin, z(64 - rdim)], axis=1)
    return jnp.concatenate([cq, sq, ck, sk], axis=1)


def _swap_halves(w):
    half = w.shape[-1] // 2
    return jnp.concatenate([w[..., half:], w[..., :half]], axis=-1)


def _prep_weights(w_in, g_q_lat, g_kv_lat, w_uq, w_uk, w_uv, dims):
    d = w_in.shape[0]
    sbw, qr, kvr, rdim = dims["sbw"], dims["qr"], dims["kvr"], dims["rdim"]
    heads, nope, vdim = dims["mla_heads"], dims["nope"], dims["vdim"]
    assert nope == 64 and rdim <= 64 and vdim == 64 and dims["sb_dim"] == 64
    o = 3 * sbw
    w_kr = w_in[:, o + qr + kvr:]
    zk = jnp.zeros((d, 64 - rdim), F32)
    kr_a = jnp.concatenate([w_kr, zk, w_kr, zk], axis=1)
    kr_b = jnp.concatenate([_swap_halves(w_kr), zk, _swap_halves(w_kr), zk], axis=1)
    w_all = jnp.concatenate([w_in[:, :o + qr + kvr], kr_a, kr_b], axis=1).astype(BF16)
    uq = w_uq.reshape(qr, heads, nope + rdim)
    zq = jnp.zeros((qr, heads, LANES - nope - rdim), F32)
    uq_a = jnp.concatenate([uq, zq], axis=2)
    uq_b = jnp.concatenate([jnp.zeros((qr, heads, nope), F32), _swap_halves(uq[..., nope:]), zq], axis=2)
    w_uq2 = jnp.concatenate([uq_a.reshape(qr, -1), uq_b.reshape(qr, -1)], axis=1).astype(BF16)
    uk = jnp.concatenate([w_uk, jnp.zeros((kvr, heads, LANES - nope), F32)], axis=2).reshape(kvr, -1)
    w_ukv = jnp.concatenate([uk, w_uv.reshape(kvr, -1)], axis=1).astype(BF16)
    place = (jnp.arange(rdim)[:, None] + 64 == jnp.arange(LANES)[None, :]).astype(BF16)
    return dict(w_all=w_all, w_uq2=w_uq2, w_ukv=w_ukv, place=place,
                g_q=g_q_lat.reshape(1, -1), g_kv=g_kv_lat.reshape(1, -1))


def _layer(x, ada, pos0, past, lw, dims, tq_sb, tq_mla, tk_sb, tk_mla, tm_in, tm_out, tt):
    b, t, d = x.shape
    n = b * t
    x2 = x.reshape(n, d)
    per_token = t < tm_in
    if per_token:
        sh1, sc1, gt1, sh2, sc2, gt2 = [jnp.repeat(a, t, axis=0)[None] for a in ada]
        tiles_in = tiles_out = tiles_tt = 1
    else:
        sh1, sc1, gt1, sh2, sc2, gt2 = [a[:, None, :] for a in ada]
        tiles_in, tiles_out, tiles_tt = t // tm_in, t // tm_out, t // tt
    tm_in, tm_out, tt = min(tm_in, n), min(tm_out, n), min(tt, n)
    heads = dims["mla_heads"]
    q_scale = (dims["nope"] + dims["rdim"]) ** -0.5
    rope_tab = _rope_table(pos0 + jnp.arange(t), dims["rdim"], q_scale)
    rope_tab = jnp.tile(rope_tab, (b, 1)) if per_token else rope_tab
    (sbq, sbk, sbv, sbkb, sbvb, ckvn, kr, qcat, kcat, vmla) = _in_proj(
        x2, sc1, sh1, lw["g_mix_pre"], rope_tab, lw, dims, tm_in, tiles_in)
    sbw, gw, vw = dims["sbw"], heads * LANES, vmla.shape[1]
    r3 = lambda a: a.reshape(b, t, a.shape[-1])
    if past is None:
        q_off, t_valid = 0, t
        k_sb, v_sb, k_m, v_m = r3(sbkb), r3(sbvb), r3(kcat), r3(vmla)
        q_sb, q_m = r3(sbq), r3(qcat)
        tq_s, tq_m = tq_sb, tq_mla
    else:
        c_k, c_v, c_ckv, c_kr = past
        p_len = c_k.shape[1]
        q_off, t_valid = p_len, p_len + t
        tq_s = tq_m = t
        kc, vc = _cache_kv(c_ckv.reshape(b * p_len, -1), c_kr.reshape(b * p_len, -1), lw["w_ukv"],
                           lw["place"], heads, tm_in)

        def with_cache(cache, new, tk):
            pad = (-(p_len + t)) % tk
            return jnp.concatenate([cache, new, jnp.zeros((b, pad, new.shape[-1]), new.dtype)], axis=1)

        k_sb = with_cache(c_k.reshape(b, p_len, sbw).astype(BF16), r3(sbkb), tk_sb)
        v_sb = with_cache(c_v.reshape(b, p_len, sbw).astype(BF16), r3(sbvb), tk_sb)
        k_m = with_cache(kc.reshape(b, p_len, gw), r3(kcat), tk_mla)
        v_m = with_cache(vc.reshape(b, p_len, vw), r3(vmla), tk_mla)
        q_sb, q_m = r3(sbq), r3(qcat)
    sbo = _sb_attn(q_sb, k_sb, v_sb, tq_s, tk_sb, q_off)
    mlao = _mla_attn(q_m, k_m, v_m, tq_m, tk_mla, q_off, t_valid)
    x1, h2, idx, gate = _out_peer(x2, sbo.reshape(n, sbw), mlao.reshape(n, vw), gt1, sc2, sh2, lw, dims,
                                  tm_out, tiles_out)
    picks = dims["peer_heads"] * PEER_TOPK
    y = _peer_ffn(idx.reshape(picks, n), h2, gate.reshape(picks, n), x1, gt2, lw["g_post_ffn"], lw["table"],
                  tt, tiles_tt)
    new = (sbk.reshape(b, t, dims["sb_heads"], dims["sb_dim"]), sbv.reshape(b, t, dims["sb_heads"], dims["sb_dim"]),
           ckvn.reshape(b, t, -1), kr.reshape(b, t, -1))
    return y.reshape(b, t, d), new


def kernel(x_prompt, x_sample, c_prompt, c_sample, cache_sb_k, cache_sb_v, cache_mla_ckv, cache_mla_krope,
           w_ada, b_ada, g_mix_pre, g_mix_post, g_ffn_pre, g_ffn_post, w_in, g_q_lat, g_kv_lat, w_uq, w_uk,
           w_uv, g_sb_out, g_mla_out, w_out, w_peer_q, sub_keys_1, sub_keys_2, u_experts, v_experts):
    depth = w_ada.shape[0]
    d = x_prompt.shape[-1]
    sb_heads, sb_dim = cache_sb_k.shape[-2:]
    kvr, mla_heads, nope = w_uk.shape[1:]
    dims = dict(sb_heads=sb_heads, sb_dim=sb_dim, sbw=sb_heads * sb_dim, qr=w_uq.shape[1], kvr=kvr,
                rdim=cache_mla_krope.shape[-1], mla_heads=mla_heads, nope=nope, vdim=w_uv.shape[-1],
                n_keys=sub_keys_1.shape[1], peer_heads=w_peer_q.shape[2] // (2 * sub_keys_1.shape[2]))
    bp, bs = c_prompt.shape[0], c_sample.shape[0]
    pad = (-(bp + bs)) % 8
    y_p, y_s = x_prompt, x_sample
    outs_p, outs_s = [], []
    for l in range(depth):
        c_all = jnp.concatenate([c_prompt, c_sample, jnp.zeros((pad, d), F32)], axis=0)
        ada = _ada(c_all, w_ada[l], b_ada[l])
        ada_p = jnp.split(ada[:bp], 6, axis=1)
        ada_s = jnp.split(ada[bp:bp + bs], 6, axis=1)
        lw = _prep_weights(w_in[l], g_q_lat[l], g_kv_lat[l], w_uq[l], w_uk[l], w_uv[l], dims)
        e = u_experts.shape[1]
        lw.update(
            g_mix_pre=g_mix_pre[l].reshape(1, d), g_post=g_mix_post[l].reshape(1, d),
            g_ffn=g_ffn_pre[l].reshape(1, d), g_post_ffn=g_ffn_post[l].reshape(1, d),
            g_sb=g_sb_out[l].reshape(1, -1), g_mla=g_mla_out[l].reshape(1, -1),
            w_out=w_out[l].astype(BF16), w_pq=w_peer_q[l].astype(BF16),
            k1=sub_keys_1[l].astype(BF16), k2=sub_keys_2[l].astype(BF16),
            table=jnp.concatenate([u_experts[l].reshape(e, d // LANES, LANES),
                                   v_experts[l].reshape(e, d // LANES, LANES)], axis=1))
        tiles = dict(tq_sb=256, tq_mla=512, tk_sb=256, tk_mla=512, tm_in=512, tm_out=256, tt=256)
        y_p, new_p = _layer(y_p, ada_p, 0, None, lw, dims, **tiles)
        past = (cache_sb_k[l], cache_sb_v[l], cache_mla_ckv[l], cache_mla_krope[l])
        y_s, new_s = _layer(y_s, ada_s, cache_sb_k.shape[2], past, lw, dims, **tiles)
        outs_p.append(new_p)
        outs_s.append(new_s)
    stack = lambda outs, k: jnp.stack([o[k] for o in outs])
    return (y_p, y_s, stack(outs_p, 0), stack(outs_p, 1), stack(outs_p, 2), stack(outs_p, 3),
            stack(outs_s, 0), stack(outs_s, 1), stack(outs_s, 2), stack(outs_s, 3))
```

```python
import functools

import jax
import jax.numpy as jnp
import numpy as np
from jax import lax
from jax.experimental import pallas as pl
from jax.experimental.pallas import tpu as pltpu

F32 = jnp.float32
BF16 = jnp.bfloat16

CHUNK = 64
ROPE_BASE = 10000.0
NORM_EPS = 1e-6
PEER_TOPK = 16
LANES = 128
VMEM_LIMIT = 56 * 1024 * 1024
SB_LOG_FLOOR = -88.0
NT_DIMS = (((1,), (1,)), ((), ()))


def _rms(x, g):
    return x * lax.rsqrt(jnp.mean(x * x, axis=-1, keepdims=True) + NORM_EPS) * g


def _cparams(sem):
    return pltpu.CompilerParams(dimension_semantics=sem, vmem_limit_bytes=VMEM_LIMIT)


def _ada_kernel(c_ref, w_ref, b_ref, o_ref):
    c = c_ref[...]
    s = c * jax.nn.sigmoid(c)
    o_ref[...] = jnp.dot(s.astype(BF16), w_ref[...].astype(BF16), preferred_element_type=F32) + b_ref[...]


def _ada(c_all, w_ada, b_ada):
    rows, d = c_all.shape
    width = w_ada.shape[1]
    return pl.pallas_call(
        _ada_kernel,
        grid=(width // d,),
        in_specs=[pl.BlockSpec((rows, d), lambda j: (0, 0)),
                  pl.BlockSpec((d, d), lambda j: (0, j)),
                  pl.BlockSpec((1, d), lambda j: (0, j))],
        out_specs=pl.BlockSpec((rows, d), lambda j: (0, j)),
        out_shape=jax.ShapeDtypeStruct((rows, width), F32),
        compiler_params=_cparams(("arbitrary",)),
        name="ada",
    )(c_all, w_ada, b_ada.reshape(1, width))


def _in_kernel(x_ref, sc_ref, sh_ref, g_ref, rope_ref, wall_ref, gq_ref, gkv_ref, wuq_ref, wukv_ref,
               sbq_o, sbk_o, sbv_o, sbkb_o, sbvb_o, ckv_o, kr_o, qcat_o, kcat_o, vmla_o,
               *, sbw, qr, kvr, rdim, heads, sb_scale):
    x = x_ref[...]
    h = _rms(x, g_ref[...]) * (1.0 + sc_ref[0]) + sh_ref[0]
    proj = jnp.dot(h.astype(BF16), wall_ref[...], preferred_element_type=F32)
    o = 0
    sbq_o[...] = (proj[:, o:o + sbw] * sb_scale).astype(BF16); o += sbw
    k = proj[:, o:o + sbw]; o += sbw
    v = proj[:, o:o + sbw]; o += sbw
    sbk_o[...] = k
    sbv_o[...] = v
    sbkb_o[...] = k.astype(BF16)
    sbvb_o[...] = v.astype(BF16)
    qlat = proj[:, o:o + qr]; o += qr
    ckv = proj[:, o:o + kvr]; o += kvr
    kr_a = proj[:, o:o + LANES]; o += LANES
    kr_b = proj[:, o:o + LANES]
    rt = rope_ref[...]
    cq, sq = rt[:, 0:LANES], rt[:, LANES:2 * LANES]
    ck, sk = rt[:, 2 * LANES:3 * LANES], rt[:, 3 * LANES:4 * LANES]
    kr128 = kr_a * ck + kr_b * sk
    kr_o[...] = kr128[:, 0:rdim]
    qn = _rms(qlat, gq_ref[...]).astype(BF16)
    q2 = jnp.dot(qn, wuq_ref[...], preferred_element_type=F32)
    gw = heads * LANES
    cq8 = jnp.concatenate([cq] * heads, axis=1)
    sq8 = jnp.concatenate([sq] * heads, axis=1)
    qcat_o[...] = (q2[:, 0:gw] * cq8 + q2[:, gw:2 * gw] * sq8).astype(BF16)
    ckvn = _rms(ckv, gkv_ref[...])
    ckv_o[...] = ckvn
    kv = jnp.dot(ckvn.astype(BF16), wukv_ref[...], preferred_element_type=F32)
    lane = lax.broadcasted_iota(jnp.int32, kr128.shape, 1)
    kr_hi = jnp.where(lane >= 64, kr128, 0.0)
    kcat_o[...] = (kv[:, 0:gw] + jnp.concatenate([kr_hi] * heads, axis=1)).astype(BF16)
    vmla_o[...] = kv[:, gw:].astype(BF16)


def _ada_spec(arr, tm, tiles_per_batch):
    d = arr.shape[-1]
    if arr.shape[1] == 1:
        return pl.BlockSpec((1, 1, d), lambda i: (i // tiles_per_batch, 0, 0))
    return pl.BlockSpec((1, tm, d), lambda i: (0, i, 0))


def _in_proj(x2, sc, sh, g_pre, rope_tab, wts, dims, tm, tiles_per_batch):
    n, d = x2.shape
    sbw, qr, kvr, rdim, heads = dims["sbw"], dims["qr"], dims["kvr"], dims["rdim"], dims["mla_heads"]
    gw = heads * LANES
    vw = wts["w_ukv"].shape[1] - gw
    full = lambda a: pl.BlockSpec(a.shape, lambda i: (0,) * a.ndim)
    row = lambda w: pl.BlockSpec((tm, w), lambda i: (i, 0))
    kern = functools.partial(_in_kernel, sbw=sbw, qr=qr, kvr=kvr, rdim=rdim, heads=heads,
                             sb_scale=dims["sb_dim"] ** -0.5)
    outs = [(sbw, BF16), (sbw, F32), (sbw, F32), (sbw, BF16), (sbw, BF16), (kvr, F32), (rdim, F32),
            (gw, BF16), (gw, BF16), (vw, BF16)]
    return pl.pallas_call(
        kern,
        grid=(n // tm,),
        in_specs=[row(d), _ada_spec(sc, tm, tiles_per_batch), _ada_spec(sh, tm, tiles_per_batch),
                  full(g_pre), pl.BlockSpec((tm, 4 * LANES), lambda i: (i % (rope_tab.shape[0] // tm), 0)),
                  full(wts["w_all"]), full(wts["g_q"]), full(wts["g_kv"]),
                  full(wts["w_uq2"]), full(wts["w_ukv"])],
        out_specs=[row(w) for w, _ in outs],
        out_shape=[jax.ShapeDtypeStruct((n, w), dt) for w, dt in outs],
        compiler_params=_cparams(("arbitrary",)),
        name="in_proj",
    )(x2, sc, sh, g_pre, rope_tab, wts["w_all"], wts["g_q"], wts["g_kv"], wts["w_uq2"], wts["w_ukv"])


def _sb_block(qm, kb, vb, c, acc, mask, tri):
    z = lax.dot_general(qm, kb, NT_DIMS, preferred_element_type=F32)
    sp = jnp.maximum(z, 0.0) + jnp.log(1.0 + jnp.exp(-jnp.abs(z)))
    lk = -sp if mask is None else jnp.where(mask, -sp, 0.0)
    hi = lk.astype(BF16)
    lo = (lk - hi.astype(F32)).astype(BF16)
    incl = jnp.dot(hi, tri, preferred_element_type=F32) + jnp.dot(lo, tri, preferred_element_type=F32)
    a = jnp.exp(z + incl + c)
    if mask is not None:
        a = jnp.where(mask, a, 0.0)
    acc = acc + jnp.dot(a.astype(BF16), vb, preferred_element_type=F32)
    return c + incl[:, 0:1], acc


def _sb_kernel(q_ref, k_ref, v_ref, o_ref, qm_scr, c_scr, acc_scr, *, tq, tk, q_off, pairs):
    i = pl.program_id(1)
    q_start = q_off + i * tq
    jm = q_start // tk
    r = lax.broadcasted_iota(jnp.int32, (tk, tk), 0)
    cc = lax.broadcasted_iota(jnp.int32, (tk, tk), 1)
    tri = (r >= cc).astype(BF16)
    qpos = q_start + lax.broadcasted_iota(jnp.int32, (tq, tk), 0)
    kpos = jm * tk + lax.broadcasted_iota(jnp.int32, (tq, tk), 1)
    causal = jnp.concatenate([kpos < qpos] * 2, axis=0)
    lane = lax.broadcasted_iota(jnp.int32, (tq, LANES), 1)
    for p in range(pairs):
        qp = q_ref[0, :, p * LANES:(p + 1) * LANES]
        zero = jnp.zeros_like(qp)
        qm_scr[p] = jnp.concatenate([jnp.where(lane < 64, qp, zero), jnp.where(lane >= 64, qp, zero)], axis=0)
    c_scr[...] = jnp.zeros_like(c_scr)
    acc_scr[...] = jnp.zeros_like(acc_scr)

    def block(j, mask):
        ks = pl.multiple_of(j * tk, tk)
        worst = None
        for p in range(pairs):
            sl = slice(p * LANES, (p + 1) * LANES)
            c, acc = _sb_block(qm_scr[p], k_ref[0, pl.ds(ks, tk), sl], v_ref[0, pl.ds(ks, tk), sl],
                               c_scr[p], acc_scr[p], mask, tri)
            c_scr[p] = c
            acc_scr[p] = acc
            worst = c if worst is None else jnp.maximum(worst, c)
        return jnp.max(worst) > SB_LOG_FLOOR

    def body(s):
        j, _ = s
        alive = block(j, None)
        return j - 1, jnp.logical_and(j > 0, alive).astype(jnp.int32)

    go0 = jnp.logical_and(jm > 0, block(jm, causal)).astype(jnp.int32)
    lax.while_loop(lambda s: s[1] > 0, body, (jm - 1, go0))
    outs = [jnp.where(lane < 64, acc_scr[p, 0:tq], acc_scr[p, tq:2 * tq]) for p in range(pairs)]
    o_ref[0] = jnp.concatenate(outs, axis=1)


def _sb_attn(q, k, v, tq, tk, q_off):
    b, t_q, w = q.shape
    t_k = k.shape[1]
    assert tk % tq == 0 and q_off % tk == 0 and q_off + t_q <= t_k
    kern = functools.partial(_sb_kernel, tq=tq, tk=tk, q_off=q_off, pairs=w // LANES)
    return pl.pallas_call(
        kern,
        grid=(b, t_q // tq),
        in_specs=[pl.BlockSpec((1, tq, w), lambda bi, i: (bi, i, 0)),
                  pl.BlockSpec((1, t_k, w), lambda bi, i: (bi, 0, 0)),
                  pl.BlockSpec((1, t_k, w), lambda bi, i: (bi, 0, 0))],
        out_specs=pl.BlockSpec((1, tq, w), lambda bi, i: (bi, i, 0)),
        out_shape=jax.ShapeDtypeStruct((b, t_q, w), F32),
        scratch_shapes=[pltpu.VMEM((w // LANES, 2 * tq, LANES), BF16),
                        pltpu.VMEM((w // LANES, 2 * tq, 1), F32),
                        pltpu.VMEM((w // LANES, 2 * tq, LANES), F32)],
        compiler_params=_cparams(("arbitrary", "arbitrary")),
        name="sb_attn",
    )(q, k, v)


def _mla_kernel(q_ref, k_ref, v_ref, o_ref, *, tq, tk, q_off, t_valid, heads):
    i = pl.program_id(1)
    q_start = q_off + i * tq
    jm = q_start // tk
    qpos = q_start + lax.broadcasted_iota(jnp.int32, (tq, tk), 0)
    kpos = jm * tk + lax.broadcasted_iota(jnp.int32, (tq, tk), 1)
    shift = CHUNK.bit_length() - 1
    visible = jnp.logical_and(jnp.right_shift(kpos, shift) <= jnp.right_shift(qpos, shift), kpos < t_valid)
    lane = lax.broadcasted_iota(jnp.int32, (tq, LANES), 1)
    outs = []
    for h in range(heads):
        sl = slice(h * LANES, (h + 1) * LANES)
        vsl = slice((h // 2) * LANES, (h // 2 + 1) * LANES)
        qh = q_ref[0, :, sl]

        def blk(j, m, l, acc, mask):
            ks = pl.multiple_of(j * tk, tk)
            s = lax.dot_general(qh, k_ref[0, pl.ds(ks, tk), sl], NT_DIMS, preferred_element_type=F32)
            if mask is not None:
                s = jnp.where(mask, s, -jnp.inf)
            m_new = jnp.maximum(m, jnp.max(s, axis=1, keepdims=True))
            alpha = jnp.exp(m - m_new)
            p = jnp.exp(s - m_new)
            l = alpha * l + jnp.sum(p, axis=1, keepdims=True)
            acc = alpha * acc + jnp.dot(p.astype(BF16), v_ref[0, pl.ds(ks, tk), vsl],
                                        preferred_element_type=F32)
            return m_new, l, acc

        m, l, acc = blk(jm, jnp.full((tq, 1), -jnp.inf, F32), jnp.zeros((tq, 1), F32),
                        jnp.zeros((tq, LANES), F32), visible)
        m, l, acc = lax.fori_loop(0, jm, lambda j, s: blk(j, *s, None), (m, l, acc))
        outs.append(acc / l)
    merged = [jnp.where(lane < 64, outs[2 * p], outs[2 * p + 1]) for p in range(heads // 2)]
    o_ref[0] = jnp.concatenate(merged, axis=1)


def _mla_attn(q, k, v, tq, tk, q_off, t_valid):
    b, t_q, gw = q.shape
    t_k = k.shape[1]
    vw = v.shape[2]
    assert tk % tq == 0 and q_off % tk == 0 and q_off + t_q <= t_k and tk % CHUNK == 0
    kern = functools.partial(_mla_kernel, tq=tq, tk=tk, q_off=q_off, t_valid=t_valid, heads=gw // LANES)
    return pl.pallas_call(
        kern,
        grid=(b, t_q // tq),
        in_specs=[pl.BlockSpec((1, tq, gw), lambda bi, i: (bi, i, 0)),
                  pl.BlockSpec((1, t_k, gw), lambda bi, i: (bi, 0, 0)),
                  pl.BlockSpec((1, t_k, vw), lambda bi, i: (bi, 0, 0))],
        out_specs=pl.BlockSpec((1, tq, vw), lambda bi, i: (bi, i, 0)),
        out_shape=jax.ShapeDtypeStruct((b, t_q, vw), F32),
        compiler_params=_cparams(("arbitrary", "arbitrary")),
        name="mla_attn",
    )(q, k, v)


def _dec_kernel(qsb_ref, knew_ref, vnew_ref, kc_ref, vc_ref, qcat_ref, ckvn_ref, krn_ref, ckvc_ref, krc_ref,
                tri_ref, wukt_ref, wuv_ref, place_ref, sbo_ref, mlao_ref,
                qm_scr, c_scr, acc_scr, qall_scr, qr_scr, m_scr, l_scr, olat_scr, alive,
                *, t, tn, p_len, pairs, heads, nblk):
    j = pl.program_id(1)
    lane = lax.broadcasted_iota(jnp.int32, (t, LANES), 1)

    def sb_blocks(get_k, get_v, mask, tri):
        worst = None
        for p in range(pairs):
            c, acc = _sb_block(qm_scr[p], get_k(p), get_v(p), c_scr[p], acc_scr[p], mask, tri)
            c_scr[p] = c
            acc_scr[p] = acc
            worst = c if worst is None else jnp.maximum(worst, c)
        alive[0] = (jnp.max(worst) > SB_LOG_FLOOR).astype(jnp.int32)

    def mla_block(ckv_b, kr_b, mask):
        s = (lax.dot_general(qall_scr[...], ckv_b, NT_DIMS, preferred_element_type=F32)
             + lax.dot_general(qr_scr[...], kr_b, NT_DIMS, preferred_element_type=F32))
        if mask is not None:
            s = jnp.where(mask, s, -jnp.inf)
        m_old = m_scr[...]
        m_new = jnp.maximum(m_old, jnp.max(s, axis=1, keepdims=True))
        alpha = jnp.exp(m_old - m_new)
        p = jnp.exp(s - m_new)
        l_scr[...] = alpha * l_scr[...] + jnp.sum(p, axis=1, keepdims=True)
        olat_scr[...] = alpha * olat_scr[...] + jnp.dot(p.astype(BF16), ckv_b, preferred_element_type=F32)
        m_scr[...] = m_new

    pair = lambda p: slice(p * LANES, (p + 1) * LANES)

    @pl.when(j == 0)
    def _():
        for p in range(pairs):
            qp = qsb_ref[0, :, pair(p)]
            zero = jnp.zeros_like(qp)
            qm_scr[p] = jnp.concatenate([jnp.where(lane < 64, qp, zero), jnp.where(lane >= 64, qp, zero)], axis=0)
        c_scr[...] = jnp.zeros_like(c_scr)
        acc_scr[...] = jnp.zeros_like(acc_scr)
        for h in range(heads):
            qh = qcat_ref[0, :, pair(h)]
            qall_scr[h * t:(h + 1) * t, :] = jnp.dot(qh, wukt_ref[h], preferred_element_type=F32).astype(BF16)
            qr_scr[h * t:(h + 1) * t, :] = lax.dot_general(qh, place_ref[...], NT_DIMS,
                                                           preferred_element_type=F32).astype(BF16)
        m_scr[...] = jnp.full_like(m_scr, -jnp.inf)
        l_scr[...] = jnp.zeros_like(l_scr)
        olat_scr[...] = jnp.zeros_like(olat_scr)
        row2 = lax.broadcasted_iota(jnp.int32, (2 * t, tn), 0)
        col2 = lax.broadcasted_iota(jnp.int32, (2 * t, tn), 1)
        causal = col2 < jnp.bitwise_and(row2, t - 1)
        sb_blocks(lambda p: knew_ref[0, :, pair(p)], lambda p: vnew_ref[0, :, pair(p)], causal,
                  tri_ref[0:tn, 0:tn])
        rows = lax.broadcasted_iota(jnp.int32, (heads * t, tn), 0)
        cols = lax.broadcasted_iota(jnp.int32, (heads * t, tn), 1)
        shift = CHUNK.bit_length() - 1
        qpos = p_len + jnp.bitwise_and(rows, t - 1)
        visible = jnp.logical_and(jnp.right_shift(p_len + cols, shift) <= jnp.right_shift(qpos, shift), cols < t)
        mla_block(ckvn_ref[0].astype(BF16), krn_ref[0].astype(BF16), visible)

    @pl.when(j > 0)
    def _():
        @pl.when(alive[0] > 0)
        def _():
            sb_blocks(lambda p: kc_ref[0, :, pair(p)].astype(BF16), lambda p: vc_ref[0, :, pair(p)].astype(BF16),
                      None, tri_ref[...])

        mla_block(ckvc_ref[0].astype(BF16), krc_ref[0].astype(BF16), None)

    @pl.when(j == nblk)
    def _():
        outs = [jnp.where(lane < 64, acc_scr[p, 0:t], acc_scr[p, t:2 * t]) for p in range(pairs)]
        sbo_ref[0] = jnp.concatenate(outs, axis=1)
        o = (olat_scr[...] / l_scr[...]).astype(BF16)
        acc = jnp.dot(o[0:t], wuv_ref[0], preferred_element_type=F32)
        for h in range(1, heads):
            acc = acc + jnp.dot(o[h * t:(h + 1) * t], wuv_ref[h], preferred_element_type=F32)
        mlao_ref[0] = acc


def _dec_attn(qsb, knew, vnew, c_k, c_v, qcat, ckvn, krn, c_ckv, c_kr, wts, tk):
    b, t, sbw = qsb.shape
    p_len = c_k.shape[1]
    kvr, rdim = c_ckv.shape[2], c_kr.shape[2]
    heads = qcat.shape[2] // LANES
    vw = wts["w_uv_pad"].shape[2]
    tn = LANES
    nblk = p_len // tk
    assert p_len % tk == 0 and t & (t - 1) == 0 and t <= tn <= tk and p_len % CHUNK == 0
    padn = lambda a: jnp.pad(a, ((0, 0), (0, tn - t), (0, 0)))
    tri = (jnp.arange(tk)[:, None] >= jnp.arange(tk)[None, :]).astype(BF16)
    new = lambda w: pl.BlockSpec((1, tn, w), lambda bi, j: (bi, 0, 0))
    qspec = lambda w: pl.BlockSpec((1, t, w), lambda bi, j: (bi, 0, 0))
    cache = lambda w: pl.BlockSpec((1, tk, w), lambda bi, j: (bi, nblk - jnp.maximum(j, 1), 0))
    full = lambda a: pl.BlockSpec(a.shape, lambda bi, j: (0,) * a.ndim)
    kern = functools.partial(_dec_kernel, t=t, tn=tn, p_len=p_len, pairs=sbw // LANES, heads=heads, nblk=nblk)
    pr = sbw // LANES
    return pl.pallas_call(
        kern,
        grid=(b, nblk + 1),
        in_specs=[qspec(sbw), new(sbw), new(sbw), cache(sbw), cache(sbw), qspec(heads * LANES),
                  new(kvr), new(rdim), cache(kvr), cache(rdim),
                  full(tri), full(wts["w_ukt_pad"]), full(wts["w_uv_pad"]), full(wts["place"])],
        out_specs=[qspec(sbw), qspec(vw)],
        out_shape=[jax.ShapeDtypeStruct((b, t, sbw), F32), jax.ShapeDtypeStruct((b, t, vw), F32)],
        scratch_shapes=[pltpu.VMEM((pr, 2 * t, LANES), BF16), pltpu.VMEM((pr, 2 * t, 1), F32),
                        pltpu.VMEM((pr, 2 * t, LANES), F32),
                        pltpu.VMEM((heads * t, kvr), BF16), pltpu.VMEM((heads * t, rdim), BF16),
                        pltpu.VMEM((heads * t, 1), F32), pltpu.VMEM((heads * t, 1), F32),
                        pltpu.VMEM((heads * t, kvr), F32), pltpu.SMEM((1,), jnp.int32)],
        compiler_params=_cparams(("arbitrary", "arbitrary")),
        name="dec_attn",
    )(qsb, padn(knew), padn(vnew), c_k, c_v, qcat, padn(ckvn), padn(krn), c_ckv, c_kr,
      tri, wts["w_ukt_pad"], wts["w_uv_pad"], wts["place"])


def _topk_rows(s, k, payload=None):
    rows = s.shape[0]
    rid = lax.broadcasted_iota(jnp.int32, s.shape, 0).astype(F32)
    vals, picks = [], []
    for _ in range(k):
        m = jnp.max(s, axis=0, keepdims=True)
        first = jnp.min(jnp.where(s == m, rid, float(rows)), axis=0, keepdims=True)
        hit = rid == first
        vals.append(m)
        if payload is None:
            picks.append(first)
        else:
            picks.append(jnp.max(jnp.where(hit, payload, -1.0), axis=0, keepdims=True))
        s = jnp.where(hit, -jnp.inf, s)
    return jnp.concatenate(vals, axis=0), jnp.concatenate(picks, axis=0)


def _staircase(k):
    return [(a, b) for a in range(k) for b in range(k) if (a + 1) * (b + 1) <= k]


def _out_kernel(x_ref, sbo_ref, mlao_ref, gt1_ref, sc2_ref, sh2_ref, gsb_ref, gmla_ref, gpost_ref, gffn_ref,
                wout_ref, wpq_ref, k1_ref, k2_ref, x1_o, h2_o, idx_o, gate_o, q_scr, *, peer_heads, n_keys):
    sb = _rms(sbo_ref[...], gsb_ref[...])
    ml = _rms(mlao_ref[...], gmla_ref[...])
    merged = jnp.concatenate([sb, ml], axis=1).astype(BF16)
    y = jnp.dot(merged, wout_ref[...], preferred_element_type=F32)
    x1 = x_ref[...] + gt1_ref[0] * _rms(y, gpost_ref[...])
    x1_o[...] = x1
    h2 = _rms(x1, gffn_ref[...]) * (1.0 + sc2_ref[0]) + sh2_ref[0]
    h2_o[...] = h2
    q = jnp.dot(h2.astype(BF16), wpq_ref[...], preferred_element_type=F32)
    half = q.shape[1] // (2 * peer_heads)
    for hd in range(peer_heads):
        for side in range(2):
            o = (2 * hd + side) * half
            q_scr[hd, side] = q[:, o:o + half].astype(BF16)
    pairs = _staircase(PEER_TOPK)
    k1 = k1_ref[...]
    k2 = k2_ref[...]

    def per_head(hd, carry):
        s1 = lax.dot_general(k1, q_scr[hd, 0], NT_DIMS, preferred_element_type=F32)
        s2 = lax.dot_general(k2, q_scr[hd, 1], NT_DIMS, preferred_element_type=F32)
        v1, i1 = _topk_rows(s1, PEER_TOPK)
        v2, i2 = _topk_rows(s2, PEER_TOPK)
        fill = (-len(pairs)) % 8
        tokens = v1.shape[1]
        cs = jnp.concatenate([v1[a:a + 1] + v2[b:b + 1] for a, b in pairs]
                             + [jnp.full((fill, tokens), -jnp.inf, F32)], axis=0)
        ci = jnp.concatenate([i1[a:a + 1] * float(n_keys) + i2[b:b + 1] for a, b in pairs]
                             + [jnp.full((fill, tokens), -1.0, F32)], axis=0)
        ts, ti = _topk_rows(cs, PEER_TOPK, payload=ci)
        ex = jnp.exp(ts - ts[0:1])
        gate_o[hd] = ex / jnp.sum(ex, axis=0, keepdims=True)
        idx_o[hd] = ti.astype(jnp.int32)
        return carry

    lax.fori_loop(0, peer_heads, per_head, 0)


def _out_peer(x2, sbo, mlao, gt1, sc2, sh2, wts, dims, tm, tiles_per_batch):
    n, d = x2.shape
    ph, nk = dims["peer_heads"], dims["n_keys"]
    half = wts["w_pq"].shape[1] // (2 * ph)
    full = lambda a: pl.BlockSpec(a.shape, lambda i: (0,) * a.ndim)
    row = lambda w: pl.BlockSpec((tm, w), lambda i: (i, 0))
    ada = lambda a: _ada_spec(a, tm, tiles_per_batch)
    names = ["g_sb", "g_mla", "g_post", "g_ffn", "w_out", "w_pq", "k1", "k2"]
    return pl.pallas_call(
        functools.partial(_out_kernel, peer_heads=ph, n_keys=nk),
        grid=(n // tm,),
        in_specs=[row(d), row(sbo.shape[1]), row(mlao.shape[1]), ada(gt1), ada(sc2), ada(sh2)]
                 + [full(wts[k]) for k in names],
        out_specs=[row(d), row(d),
                   pl.BlockSpec((ph, PEER_TOPK, tm), lambda i: (0, 0, i)),
                   pl.BlockSpec((ph, PEER_TOPK, tm), lambda i: (0, 0, i))],
        out_shape=[jax.ShapeDtypeStruct((n, d), F32), jax.ShapeDtypeStruct((n, d), F32),
                   jax.ShapeDtypeStruct((ph, PEER_TOPK, n), jnp.int32),
                   jax.ShapeDtypeStruct((ph, PEER_TOPK, n), F32)],
        scratch_shapes=[pltpu.VMEM((ph, 2, tm, half), BF16)],
        compiler_params=_cparams(("arbitrary",)),
        name="out_peer",
    )(x2, sbo, mlao, gt1, sc2, sh2, *[wts[k] for k in names])


def _gelu(x):
    return 0.5 * x * (1.0 + lax.erf(x * (2.0 ** -0.5)))


def _split_bf16(x):
    hi = x.astype(BF16)
    return hi, (x - hi.astype(F32)).astype(BF16)


def _peer_kernel(idx_hbm, slot_hbm, tab_hbm, h2_ref, gate_ref, x1_ref, gt2_ref, gpost_ref, y_o,
                 idx_smem, gate_scr, f_scr, idx_sem, row_sem, *bufs,
                 tt, picks, chunks, grp, nsets, ntiles):
    tile = pl.program_id(0)
    par = tile % 2
    ahead = (nsets - 1) * grp
    n_it = tt // (grp * nsets)

    def idx_copy(tl, half):
        return pltpu.make_async_copy(idx_hbm.at[tl], idx_smem.at[half], idx_sem.at[half])

    def row_copy(e, slot, k):
        return pltpu.make_async_copy(tab_hbm.at[e], bufs[slot].at[:, k, :], row_sem.at[slot])

    def issue(half, t, slot):
        for k in range(picks):
            row_copy(idx_smem[half, t, k], slot, k).start(priority=k % 2)

    def wait_slot(slot):
        pltpu.make_async_copy(slot_hbm.at[0], bufs[slot], row_sem.at[slot]).wait()

    @pl.when(tile == 0)
    def _():
        first = idx_copy(0, 0)
        first.start()
        first.wait()
        for t in range(ahead):
            def one(k, carry, t=t):
                row_copy(idx_smem[0, t, k], t, k).start()
                return carry
            lax.fori_loop(0, picks, one, 0)

    nxt = idx_copy((tile + 1) % ntiles, 1 - par)
    nxt.start()
    gate_scr[...] = gate_ref[...].T

    ones8 = jnp.ones((8, LANES), BF16)
    eye = (lax.broadcasted_iota(jnp.int32, (picks, picks), 0)
           == lax.broadcasted_iota(jnp.int32, (picks, picks), 1)).astype(BF16)

    def compute(slot, t):
        buf = bufs[slot]
        xt = h2_ref[pl.ds(t, 1), :]
        part = buf[0] * xt[:, 0:LANES]
        for c in range(1, chunks):
            part = part + buf[c] * xt[:, c * LANES:(c + 1) * LANES]
        hi, lo = _split_bf16(part)
        s8 = (lax.dot_general(ones8, hi, NT_DIMS, preferred_element_type=F32)
              + lax.dot_general(ones8, lo, NT_DIMS, preferred_element_type=F32))
        a8 = _gelu(s8) * gate_scr[pl.ds(t, 1), :]
        ah, al = _split_bf16(a8)
        reps = picks // 8
        acol = (lax.dot_general(eye, jnp.concatenate([ah] * reps, axis=0), NT_DIMS, preferred_element_type=F32)
                + lax.dot_general(eye, jnp.concatenate([al] * reps, axis=0), NT_DIMS, preferred_element_type=F32))
        outs = [jnp.sum(acol * buf[chunks + c], axis=0, keepdims=True) for c in range(chunks)]
        f_scr[pl.ds(t, 1), :] = jnp.concatenate(outs, axis=1)

    def body(it, carry):
        @pl.when(it == n_it - 1)
        def _():
            nxt.wait()

        for s in range(nsets):
            t0 = (it * nsets + s) * grp
            for j in range(grp):
                wait_slot(s * grp + j)
            for j in range(grp):
                la = t0 + j + ahead
                cross = la >= tt
                issue(jnp.where(cross, 1 - par, par), jnp.where(cross, la - tt, la),
                      ((s + nsets - 1) % nsets) * grp + j)
            for j in range(grp):
                compute(s * grp + j, t0 + j)
        return carry

    lax.fori_loop(0, n_it, body, 0)

    @pl.when(tile == ntiles - 1)
    def _():
        for slot in range(ahead):
            wait_slot(slot)

    y_o[...] = x1_ref[...] + gt2_ref[0] * _rms(f_scr[...], gpost_ref[...])


def _peer_ffn(idx, h2, gate, x1, gt2, g_post, table, tt, tiles_per_batch, grp=16, nsets=2):
    n, d = h2.shape
    picks = gate.shape[0]
    chunks = d // LANES
    ntiles = n // tt
    assert table.shape[1:] == (2 * chunks, LANES) and tt % (grp * nsets) == 0 and picks % 8 == 0
    assert table.shape[0] % picks == 0
    idx_tiles = idx.reshape(picks, ntiles, tt).transpose(1, 2, 0)
    slot_view = table.reshape(-1, 2 * chunks, picks, LANES)
    kern = functools.partial(_peer_kernel, tt=tt, picks=picks, chunks=chunks, grp=grp, nsets=nsets,
                             ntiles=ntiles)
    row = lambda w: pl.BlockSpec((tt, w), lambda i: (i, 0))
    hbm = pl.BlockSpec(memory_space=pl.ANY)
    return pl.pallas_call(
        kern,
        grid=(ntiles,),
        in_specs=[hbm, hbm, hbm, row(d), pl.BlockSpec((picks, tt), lambda i: (0, i)), row(d),
                  _ada_spec(gt2, tt, tiles_per_batch), pl.BlockSpec(g_post.shape, lambda i: (0, 0))],
        out_specs=row(d),
        out_shape=jax.ShapeDtypeStruct((n, d), F32),
        scratch_shapes=[pltpu.SMEM((2, tt, picks), jnp.int32),
                        pltpu.VMEM((tt, picks), F32),
                        pltpu.VMEM((tt, d), F32),
                        pltpu.SemaphoreType.DMA((2,)),
                        pltpu.SemaphoreType.DMA((grp * nsets,))]
                       + [pltpu.VMEM((2 * chunks, picks, LANES), F32) for _ in range(grp * nsets)],
        compiler_params=_cparams(("arbitrary",)),
        name="peer_ffn",
    )(idx_tiles, slot_view, table, h2, gate, x1, gt2, g_post)


def _rope_table(pos, rdim, q_scale):
    half = rdim // 2
    inv = ROPE_BASE ** (-jnp.arange(half, dtype=F32) / half)
    ang = pos.astype(F32)[:, None] * inv[None, :]
    cos = jnp.concatenate([jnp.cos(ang), jnp.cos(ang)], axis=1)
    sin = jnp.concatenate([-jnp.sin(ang), jnp.sin(ang)], axis=1)
    t = pos.shape[0]
    z = lambda w: jnp.zeros((t, w), F32)
    cq = jnp.concatenate([jnp.ones((t, 64), F32), cos, z(64 - rdim)], axis=1) * q_scale
    sq = jnp.concatenate([z(64), sin, z(64 - rdim)], axis=1) * q_scale
    ck = jnp.concatenate([cos, z(64 - rdim), cos, z(64 - rdim)], axis=1)
    sk = jnp.concatenate([sin, z(64 - rdim), sin, z(64 - rdim)], axis=1)
    return jnp.concatenate([cq, sq, ck, sk], axis=1)


def _swap_halves(w):
    half = w.shape[-1] // 2
    return jnp.concatenate([w[..., half:], w[..., :half]], axis=-1)


def _prep_weights(w_in, g_q_lat, g_kv_lat, w_uq, w_uk, w_uv, dims):
    d = w_in.shape[0]
    sbw, qr, kvr, rdim = dims["sbw"], dims["qr"], dims["kvr"], dims["rdim"]
    heads, nope, vdim = dims["mla_heads"], dims["nope"], dims["vdim"]
    assert nope == 64 and rdim <= 64 and vdim == 64 and dims["sb_dim"] == 64
    o = 3 * sbw
    w_kr = w_in[:, o + qr + kvr:]
    zk = jnp.zeros((d, 64 - rdim), F32)
    kr_a = jnp.concatenate([w_kr, zk, w_kr, zk], axis=1)
    kr_b = jnp.concatenate([_swap_halves(w_kr), zk, _swap_halves(w_kr), zk], axis=1)
    w_all = jnp.concatenate([w_in[:, :o + qr + kvr], kr_a, kr_b], axis=1).astype(BF16)
    uq = w_uq.reshape(qr, heads, nope + rdim)
    zq = jnp.zeros((qr, heads, LANES - nope - rdim), F32)
    uq_a = jnp.concatenate([uq, zq], axis=2)
    uq_b = jnp.concatenate([jnp.zeros((qr, heads, nope), F32), _swap_halves(uq[..., nope:]), zq], axis=2)
    w_uq2 = jnp.concatenate([uq_a.reshape(qr, -1), uq_b.reshape(qr, -1)], axis=1).astype(BF16)
    uk = jnp.concatenate([w_uk, jnp.zeros((kvr, heads, LANES - nope), F32)], axis=2).reshape(kvr, -1)
    w_ukv = jnp.concatenate([uk, w_uv.reshape(kvr, -1)], axis=1).astype(BF16)
    place = (jnp.arange(rdim)[:, None] + 64 == jnp.arange(LANES)[None, :]).astype(BF16)
    w_ukt_pad = jnp.pad(jnp.transpose(w_uk, (1, 2, 0)), ((0, 0), (0, LANES - nope), (0, 0))).astype(BF16)
    head_cols = (jnp.arange(heads)[:, None, None] == jnp.arange(heads)[None, :, None]) * jnp.ones((1, 1, vdim), F32)
    w_uv_pad = (w_uv[None] * head_cols[:, None]).reshape(heads, kvr, heads * vdim).astype(BF16)
    return dict(w_all=w_all, w_uq2=w_uq2, w_ukv=w_ukv, place=place, w_ukt_pad=w_ukt_pad, w_uv_pad=w_uv_pad,
                g_q=g_q_lat.reshape(1, -1), g_kv=g_kv_lat.reshape(1, -1))


def _layer(x, ada, pos0, past, lw, dims, tq_sb, tq_mla, tk_sb, tk_mla, tm_in, tm_out, tt):
    b, t, d = x.shape
    n = b * t
    x2 = x.reshape(n, d)
    per_token = t < tm_in
    if per_token:
        sh1, sc1, gt1, sh2, sc2, gt2 = [jnp.repeat(a, t, axis=0)[None] for a in ada]
        tiles_in = tiles_out = tiles_tt = 1
    else:
        sh1, sc1, gt1, sh2, sc2, gt2 = [a[:, None, :] for a in ada]
        tiles_in, tiles_out, tiles_tt = t // tm_in, t // tm_out, t // tt
    tm_in, tm_out, tt = min(tm_in, n), min(tm_out, n), min(tt, n)
    heads = dims["mla_heads"]
    q_scale = (dims["nope"] + dims["rdim"]) ** -0.5
    rope_tab = _rope_table(pos0 + jnp.arange(t), dims["rdim"], q_scale)
    rope_tab = jnp.tile(rope_tab, (b, 1)) if per_token else rope_tab
    (sbq, sbk, sbv, sbkb, sbvb, ckvn, kr, qcat, kcat, vmla) = _in_proj(
        x2, sc1, sh1, lw["g_mix_pre"], rope_tab, lw, dims, tm_in, tiles_in)
    sbw, gw, vw = dims["sbw"], heads * LANES, vmla.shape[1]
    r3 = lambda a: a.reshape(b, t, a.shape[-1])
    if past is None:
        sbo = _sb_attn(r3(sbq), r3(sbkb), r3(sbvb), tq_sb, tk_sb, 0)
        mlao = _mla_attn(r3(qcat), r3(kcat), r3(vmla), tq_mla, tk_mla, 0, t)
    else:
        c_k, c_v, c_ckv, c_kr = past
        p_len = c_k.shape[1]
        sbo, mlao = _dec_attn(r3(sbq), r3(sbkb), r3(sbvb), c_k.reshape(b, p_len, sbw), c_v.reshape(b, p_len, sbw),
                              r3(qcat), r3(ckvn), r3(kr), c_ckv, c_kr, lw, tk_mla)
    x1, h2, idx, gate = _out_peer(x2, sbo.reshape(n, sbw), mlao.reshape(n, vw), gt1, sc2, sh2, lw, dims,
                                  tm_out, tiles_out)
    picks = dims["peer_heads"] * PEER_TOPK
    y = _peer_ffn(idx.reshape(picks, n), h2, gate.reshape(picks, n), x1, gt2, lw["g_post_ffn"], lw["table"],
                  tt, tiles_tt)
    new = (sbk.reshape(b, t, dims["sb_heads"], dims["sb_dim"]), sbv.reshape(b, t, dims["sb_heads"], dims["sb_dim"]),
           ckvn.reshape(b, t, -1), kr.reshape(b, t, -1))
    return y.reshape(b, t, d), new


def kernel(x_prompt, x_sample, c_prompt, c_sample, cache_sb_k, cache_sb_v, cache_mla_ckv, cache_mla_krope,
           w_ada, b_ada, g_mix_pre, g_mix_post, g_ffn_pre, g_ffn_post, w_in, g_q_lat, g_kv_lat, w_uq, w_uk,
           w_uv, g_sb_out, g_mla_out, w_out, w_peer_q, sub_keys_1, sub_keys_2, u_experts, v_experts):
    depth = w_ada.shape[0]
    d = x_prompt.shape[-1]
    sb_heads, sb_dim = cache_sb_k.shape[-2:]
    kvr, mla_heads, nope = w_uk.shape[1:]
    dims = dict(sb_heads=sb_heads, sb_dim=sb_dim, sbw=sb_heads * sb_dim, qr=w_uq.shape[1], kvr=kvr,
                rdim=cache_mla_krope.shape[-1], mla_heads=mla_heads, nope=nope, vdim=w_uv.shape[-1],
                n_keys=sub_keys_1.shape[1], peer_heads=w_peer_q.shape[2] // (2 * sub_keys_1.shape[2]))
    bp, bs = c_prompt.shape[0], c_sample.shape[0]
    pad = (-(bp + bs)) % 8
    y_p, y_s = x_prompt, x_sample
    outs_p, outs_s = [], []
    for l in range(depth):
        c_all = jnp.concatenate([c_prompt, c_sample, jnp.zeros((pad, d), F32)], axis=0)
        ada = _ada(c_all, w_ada[l], b_ada[l])
        ada_p = jnp.split(ada[:bp], 6, axis=1)
        ada_s = jnp.split(ada[bp:bp + bs], 6, axis=1)
        lw = _prep_weights(w_in[l], g_q_lat[l], g_kv_lat[l], w_uq[l], w_uk[l], w_uv[l], dims)
        e = u_experts.shape[1]
        lw.update(
            g_mix_pre=g_mix_pre[l].reshape(1, d), g_post=g_mix_post[l].reshape(1, d),
            g_ffn=g_ffn_pre[l].reshape(1, d), g_post_ffn=g_ffn_post[l].reshape(1, d),
            g_sb=g_sb_out[l].reshape(1, -1), g_mla=g_mla_out[l].reshape(1, -1),
            w_out=w_out[l].astype(BF16), w_pq=w_peer_q[l].astype(BF16),
            k1=sub_keys_1[l].astype(BF16), k2=sub_keys_2[l].astype(BF16),
            table=jnp.concatenate([u_experts[l].reshape(e, d // LANES, LANES),
                                   v_experts[l].reshape(e, d // LANES, LANES)], axis=1))
        tiles = dict(tq_sb=256, tq_mla=512, tk_sb=256, tk_mla=512, tm_in=512, tm_out=256, tt=256)
        y_p, new_p = _layer(y_p, ada_p, 0, None, lw, dims, **tiles)
        past = (cache_sb_k[l], cache_sb_v[l], cache_mla_ckv[l], cache_mla_krope[l])
        y_s, new_s = _layer(y_s, ada_s, cache_sb_k.shape[2], past, lw, dims, **tiles)
        outs_p.append(new_p)
        outs_s.append(new_s)
    stack = lambda outs, k: jnp.stack([o[k] for o in outs])
    return (y_p, y_s, stack(outs_p, 0), stack(outs_p, 1), stack(outs_p, 2), stack(outs_p, 3),
            stack(outs_s, 0), stack(outs_s, 1), stack(outs_s, 2), stack(outs_s, 3))
```

```python
import functools

import jax
import jax.numpy as jnp
import numpy as np
from jax import lax
from jax.experimental import pallas as pl
from jax.experimental.pallas import tpu as pltpu

F32 = jnp.float32
BF16 = jnp.bfloat16

CHUNK = 64
ROPE_BASE = 10000.0
NORM_EPS = 1e-6
PEER_TOPK = 16
LANES = 128
VMEM_LIMIT = 56 * 1024 * 1024
SB_LOG_FLOOR = -88.0
NT_DIMS = (((1,), (1,)), ((), ()))


def _rms(x, g):
    return x * lax.rsqrt(jnp.mean(x * x, axis=-1, keepdims=True) + NORM_EPS) * g


def _cparams(sem):
    return pltpu.CompilerParams(dimension_semantics=sem, vmem_limit_bytes=VMEM_LIMIT)


def _ada_kernel(c_ref, w_ref, b_ref, o_ref):
    c = c_ref[...]
    s = c * jax.nn.sigmoid(c)
    o_ref[...] = jnp.dot(s.astype(BF16), w_ref[...].astype(BF16), preferred_element_type=F32) + b_ref[...]


def _ada(c_all, w_ada, b_ada):
    rows, d = c_all.shape
    width = w_ada.shape[1]
    return pl.pallas_call(
        _ada_kernel,
        grid=(width // d,),
        in_specs=[pl.BlockSpec((rows, d), lambda j: (0, 0)),
                  pl.BlockSpec((d, d), lambda j: (0, j)),
                  pl.BlockSpec((1, d), lambda j: (0, j))],
        out_specs=pl.BlockSpec((rows, d), lambda j: (0, j)),
        out_shape=jax.ShapeDtypeStruct((rows, width), F32),
        compiler_params=_cparams(("arbitrary",)),
        name="ada",
    )(c_all, w_ada, b_ada.reshape(1, width))


def _in_kernel(x_ref, sc_ref, sh_ref, g_ref, rope_ref, wall_ref, gq_ref, gkv_ref, wuq_ref, wukv_ref,
               sbq_o, sbk_o, sbv_o, sbkb_o, sbvb_o, ckv_o, kr_o, qcat_o, kcat_o, vmla_o,
               *, sbw, qr, kvr, rdim, heads, sb_scale):
    x = x_ref[...]
    h = _rms(x, g_ref[...]) * (1.0 + sc_ref[0]) + sh_ref[0]
    proj = jnp.dot(h.astype(BF16), wall_ref[...], preferred_element_type=F32)
    o = 0
    sbq_o[...] = (proj[:, o:o + sbw] * sb_scale).astype(BF16); o += sbw
    k = proj[:, o:o + sbw]; o += sbw
    v = proj[:, o:o + sbw]; o += sbw
    sbk_o[...] = k
    sbv_o[...] = v
    sbkb_o[...] = k.astype(BF16)
    sbvb_o[...] = v.astype(BF16)
    qlat = proj[:, o:o + qr]; o += qr
    ckv = proj[:, o:o + kvr]; o += kvr
    kr_a = proj[:, o:o + LANES]; o += LANES
    kr_b = proj[:, o:o + LANES]
    rt = rope_ref[...]
    cq, sq = rt[:, 0:LANES], rt[:, LANES:2 * LANES]
    ck, sk = rt[:, 2 * LANES:3 * LANES], rt[:, 3 * LANES:4 * LANES]
    kr128 = kr_a * ck + kr_b * sk
    kr_o[...] = kr128[:, 0:rdim]
    qn = _rms(qlat, gq_ref[...]).astype(BF16)
    q2 = jnp.dot(qn, wuq_ref[...], preferred_element_type=F32)
    gw = heads * LANES
    cq8 = jnp.concatenate([cq] * heads, axis=1)
    sq8 = jnp.concatenate([sq] * heads, axis=1)
    qcat_o[...] = (q2[:, 0:gw] * cq8 + q2[:, gw:2 * gw] * sq8).astype(BF16)
    ckvn = _rms(ckv, gkv_ref[...])
    ckv_o[...] = ckvn
    kv = jnp.dot(ckvn.astype(BF16), wukv_ref[...], preferred_element_type=F32)
    lane = lax.broadcasted_iota(jnp.int32, kr128.shape, 1)
    kr_hi = jnp.where(lane >= 64, kr128, 0.0)
    kcat_o[...] = (kv[:, 0:gw] + jnp.concatenate([kr_hi] * heads, axis=1)).astype(BF16)
    vmla_o[...] = kv[:, gw:].astype(BF16)


def _ada_spec(arr, tm, tiles_per_batch):
    d = arr.shape[-1]
    if arr.shape[1] == 1:
        return pl.BlockSpec((1, 1, d), lambda i: (i // tiles_per_batch, 0, 0))
    return pl.BlockSpec((1, tm, d), lambda i: (0, i, 0))


def _in_proj(x2, sc, sh, g_pre, rope_tab, wts, dims, tm, tiles_per_batch):
    n, d = x2.shape
    sbw, qr, kvr, rdim, heads = dims["sbw"], dims["qr"], dims["kvr"], dims["rdim"], dims["mla_heads"]
    gw = heads * LANES
    vw = wts["w_ukv"].shape[1] - gw
    full = lambda a: pl.BlockSpec(a.shape, lambda i: (0,) * a.ndim)
    row = lambda w: pl.BlockSpec((tm, w), lambda i: (i, 0))
    kern = functools.partial(_in_kernel, sbw=sbw, qr=qr, kvr=kvr, rdim=rdim, heads=heads,
                             sb_scale=dims["sb_dim"] ** -0.5)
    outs = [(sbw, BF16), (sbw, F32), (sbw, F32), (sbw, BF16), (sbw, BF16), (kvr, F32), (rdim, F32),
            (gw, BF16), (gw, BF16), (vw, BF16)]
    return pl.pallas_call(
        kern,
        grid=(n // tm,),
        in_specs=[row(d), _ada_spec(sc, tm, tiles_per_batch), _ada_spec(sh, tm, tiles_per_batch),
                  full(g_pre), pl.BlockSpec((tm, 4 * LANES), lambda i: (i % (rope_tab.shape[0] // tm), 0)),
                  full(wts["w_all"]), full(wts["g_q"]), full(wts["g_kv"]),
                  full(wts["w_uq2"]), full(wts["w_ukv"])],
        out_specs=[row(w) for w, _ in outs],
        out_shape=[jax.ShapeDtypeStruct((n, w), dt) for w, dt in outs],
        compiler_params=_cparams(("arbitrary",)),
        name="in_proj",
    )(x2, sc, sh, g_pre, rope_tab, wts["w_all"], wts["g_q"], wts["g_kv"], wts["w_uq2"], wts["w_ukv"])


def _sb_block(qm, kb, vb, c, acc, mask, tri):
    z = lax.dot_general(qm, kb, NT_DIMS, preferred_element_type=F32)
    sp = jnp.maximum(z, 0.0) + jnp.log(1.0 + jnp.exp(-jnp.abs(z)))
    lk = -sp if mask is None else jnp.where(mask, -sp, 0.0)
    hi = lk.astype(BF16)
    lo = (lk - hi.astype(F32)).astype(BF16)
    incl = jnp.dot(hi, tri, preferred_element_type=F32) + jnp.dot(lo, tri, preferred_element_type=F32)
    a = jnp.exp(z + incl + c)
    if mask is not None:
        a = jnp.where(mask, a, 0.0)
    acc = acc + jnp.dot(a.astype(BF16), vb, preferred_element_type=F32)
    return c + incl[:, 0:1], acc


def _sb_kernel(q_ref, k_ref, v_ref, o_ref, qm_scr, c_scr, acc_scr, *, tq, tk, q_off, pairs):
    i = pl.program_id(1)
    q_start = q_off + i * tq
    jm = q_start // tk
    r = lax.broadcasted_iota(jnp.int32, (tk, tk), 0)
    cc = lax.broadcasted_iota(jnp.int32, (tk, tk), 1)
    tri = (r >= cc).astype(BF16)
    qpos = q_start + lax.broadcasted_iota(jnp.int32, (tq, tk), 0)
    kpos = jm * tk + lax.broadcasted_iota(jnp.int32, (tq, tk), 1)
    causal = jnp.concatenate([kpos < qpos] * 2, axis=0)
    lane = lax.broadcasted_iota(jnp.int32, (tq, LANES), 1)
    for p in range(pairs):
        qp = q_ref[0, :, p * LANES:(p + 1) * LANES]
        zero = jnp.zeros_like(qp)
        qm_scr[p] = jnp.concatenate([jnp.where(lane < 64, qp, zero), jnp.where(lane >= 64, qp, zero)], axis=0)
    c_scr[...] = jnp.zeros_like(c_scr)
    acc_scr[...] = jnp.zeros_like(acc_scr)

    def block(j, mask):
        ks = pl.multiple_of(j * tk, tk)
        worst = None
        for p in range(pairs):
            sl = slice(p * LANES, (p + 1) * LANES)
            c, acc = _sb_block(qm_scr[p], k_ref[0, pl.ds(ks, tk), sl], v_ref[0, pl.ds(ks, tk), sl],
                               c_scr[p], acc_scr[p], mask, tri)
            c_scr[p] = c
            acc_scr[p] = acc
            worst = c if worst is None else jnp.maximum(worst, c)
        return jnp.max(worst) > SB_LOG_FLOOR

    def body(s):
        j, _ = s
        alive = block(j, None)
        return j - 1, jnp.logical_and(j > 0, alive).astype(jnp.int32)

    go0 = jnp.logical_and(jm > 0, block(jm, causal)).astype(jnp.int32)
    lax.while_loop(lambda s: s[1] > 0, body, (jm - 1, go0))
    outs = [jnp.where(lane < 64, acc_scr[p, 0:tq], acc_scr[p, tq:2 * tq]) for p in range(pairs)]
    o_ref[0] = jnp.concatenate(outs, axis=1)


def _sb_attn(q, k, v, tq, tk, q_off):
    b, t_q, w = q.shape
    t_k = k.shape[1]
    assert tk % tq == 0 and q_off % tk == 0 and q_off + t_q <= t_k
    kern = functools.partial(_sb_kernel, tq=tq, tk=tk, q_off=q_off, pairs=w // LANES)
    return pl.pallas_call(
        kern,
        grid=(b, t_q // tq),
        in_specs=[pl.BlockSpec((1, tq, w), lambda bi, i: (bi, i, 0)),
                  pl.BlockSpec((1, t_k, w), lambda bi, i: (bi, 0, 0)),
                  pl.BlockSpec((1, t_k, w), lambda bi, i: (bi, 0, 0))],
        out_specs=pl.BlockSpec((1, tq, w), lambda bi, i: (bi, i, 0)),
        out_shape=jax.ShapeDtypeStruct((b, t_q, w), F32),
        scratch_shapes=[pltpu.VMEM((w // LANES, 2 * tq, LANES), BF16),
                        pltpu.VMEM((w // LANES, 2 * tq, 1), F32),
                        pltpu.VMEM((w // LANES, 2 * tq, LANES), F32)],
        compiler_params=_cparams(("arbitrary", "arbitrary")),
        name="sb_attn",
    )(q, k, v)


def _mla_kernel(q_ref, k_ref, v_ref, o_ref, *, tq, tk, q_off, t_valid, heads):
    i = pl.program_id(1)
    q_start = q_off + i * tq
    jm = q_start // tk
    qpos = q_start + lax.broadcasted_iota(jnp.int32, (tq, tk), 0)
    kpos = jm * tk + lax.broadcasted_iota(jnp.int32, (tq, tk), 1)
    shift = CHUNK.bit_length() - 1
    visible = jnp.logical_and(jnp.right_shift(kpos, shift) <= jnp.right_shift(qpos, shift), kpos < t_valid)
    lane = lax.broadcasted_iota(jnp.int32, (tq, LANES), 1)
    outs = []
    for h in range(heads):
        sl = slice(h * LANES, (h + 1) * LANES)
        vsl = slice((h // 2) * LANES, (h // 2 + 1) * LANES)
        qh = q_ref[0, :, sl]

        def blk(j, m, l, acc, mask):
            ks = pl.multiple_of(j * tk, tk)
            s = lax.dot_general(qh, k_ref[0, pl.ds(ks, tk), sl], NT_DIMS, preferred_element_type=F32)
            if mask is not None:
                s = jnp.where(mask, s, -jnp.inf)
            m_new = jnp.maximum(m, jnp.max(s, axis=1, keepdims=True))
            alpha = jnp.exp(m - m_new)
            p = jnp.exp(s - m_new)
            l = alpha * l + jnp.sum(p, axis=1, keepdims=True)
            acc = alpha * acc + jnp.dot(p.astype(BF16), v_ref[0, pl.ds(ks, tk), vsl],
                                        preferred_element_type=F32)
            return m_new, l, acc

        m, l, acc = blk(jm, jnp.full((tq, 1), -jnp.inf, F32), jnp.zeros((tq, 1), F32),
                        jnp.zeros((tq, LANES), F32), visible)
        m, l, acc = lax.fori_loop(0, jm, lambda j, s: blk(j, *s, None), (m, l, acc))
        outs.append(acc / l)
    merged = [jnp.where(lane < 64, outs[2 * p], outs[2 * p + 1]) for p in range(heads // 2)]
    o_ref[0] = jnp.concatenate(merged, axis=1)


def _mla_attn(q, k, v, tq, tk, q_off, t_valid):
    b, t_q, gw = q.shape
    t_k = k.shape[1]
    vw = v.shape[2]
    assert tk % tq == 0 and q_off % tk == 0 and q_off + t_q <= t_k and tk % CHUNK == 0
    kern = functools.partial(_mla_kernel, tq=tq, tk=tk, q_off=q_off, t_valid=t_valid, heads=gw // LANES)
    return pl.pallas_call(
        kern,
        grid=(b, t_q // tq),
        in_specs=[pl.BlockSpec((1, tq, gw), lambda bi, i: (bi, i, 0)),
                  pl.BlockSpec((1, t_k, gw), lambda bi, i: (bi, 0, 0)),
                  pl.BlockSpec((1, t_k, vw), lambda bi, i: (bi, 0, 0))],
        out_specs=pl.BlockSpec((1, tq, vw), lambda bi, i: (bi, i, 0)),
        out_shape=jax.ShapeDtypeStruct((b, t_q, vw), F32),
        compiler_params=_cparams(("arbitrary", "arbitrary")),
        name="mla_attn",
    )(q, k, v)


def _dec_kernel(qsb_ref, knew_ref, vnew_ref, kc_ref, vc_ref, qcat_ref, ckvn_ref, krn_ref, ckvc_ref, krc_ref,
                tri_ref, wukt_ref, wuv_ref, place_ref, sbo_ref, mlao_ref,
                qh_scr, c_scr, acc_scr, qall_scr, qr_scr, m_scr, l_scr, olat_scr, alive,
                *, t, tn, p_len, sb_heads, sb_dim, heads, nblk):
    j = pl.program_id(1)
    hrows = sb_heads * t

    def sb_blocks(get_k, get_v, mask, tri):
        z = jnp.concatenate([lax.dot_general(qh_scr[h], get_k(h), NT_DIMS, preferred_element_type=F32)
                             for h in range(sb_heads)], axis=0)
        sp = jnp.maximum(z, 0.0) + jnp.log(1.0 + jnp.exp(-jnp.abs(z)))
        lk = -sp if mask is None else jnp.where(mask, -sp, 0.0)
        hi, lo = _split_bf16(lk)
        both = jnp.dot(jnp.concatenate([hi, lo], axis=0), tri, preferred_element_type=F32)
        incl = both[0:hrows] + both[hrows:2 * hrows]
        a = jnp.exp(z + incl + c_scr[...])
        if mask is not None:
            a = jnp.where(mask, a, 0.0)
        a = a.astype(BF16)
        for h in range(sb_heads):
            acc_scr[h] = acc_scr[h] + jnp.dot(a[h * t:(h + 1) * t], get_v(h), preferred_element_type=F32)
        c = c_scr[...] + incl[:, 0:1]
        c_scr[...] = c
        alive[0] = (jnp.max(c) > SB_LOG_FLOOR).astype(jnp.int32)

    def mla_block(ckv_b, kr_b, mask):
        s = (lax.dot_general(qall_scr[...], ckv_b, NT_DIMS, preferred_element_type=F32)
             + lax.dot_general(qr_scr[...], kr_b, NT_DIMS, preferred_element_type=F32))
        if mask is not None:
            s = jnp.where(mask, s, -jnp.inf)
        m_old = m_scr[...]
        m_new = jnp.maximum(m_old, jnp.max(s, axis=1, keepdims=True))
        alpha = jnp.exp(m_old - m_new)
        p = jnp.exp(s - m_new)
        l_scr[...] = alpha * l_scr[...] + jnp.sum(p, axis=1, keepdims=True)
        olat_scr[...] = alpha * olat_scr[...] + jnp.dot(p.astype(BF16), ckv_b, preferred_element_type=F32)
        m_scr[...] = m_new

    pair = lambda p: slice(p * LANES, (p + 1) * LANES)

    head = lambda h: slice(h * sb_dim, (h + 1) * sb_dim)

    @pl.when(j == 0)
    def _():
        for h in range(sb_heads):
            qh_scr[h] = qsb_ref[0, :, head(h)]
        c_scr[...] = jnp.zeros_like(c_scr)
        acc_scr[...] = jnp.zeros_like(acc_scr)
        for h in range(heads):
            qh = qcat_ref[0, :, pair(h)]
            qall_scr[h * t:(h + 1) * t, :] = jnp.dot(qh, wukt_ref[h], preferred_element_type=F32).astype(BF16)
            qr_scr[h * t:(h + 1) * t, :] = lax.dot_general(qh, place_ref[...], NT_DIMS,
                                                           preferred_element_type=F32).astype(BF16)
        m_scr[...] = jnp.full_like(m_scr, -jnp.inf)
        l_scr[...] = jnp.zeros_like(l_scr)
        olat_scr[...] = jnp.zeros_like(olat_scr)
        row2 = lax.broadcasted_iota(jnp.int32, (hrows, tn), 0)
        col2 = lax.broadcasted_iota(jnp.int32, (hrows, tn), 1)
        causal = col2 < jnp.bitwise_and(row2, t - 1)
        sb_blocks(lambda h: knew_ref[0, :, head(h)], lambda h: vnew_ref[0, :, head(h)], causal,
                  tri_ref[0:tn, 0:tn])
        rows = lax.broadcasted_iota(jnp.int32, (heads * t, tn), 0)
        cols = lax.broadcasted_iota(jnp.int32, (heads * t, tn), 1)
        shift = CHUNK.bit_length() - 1
        qpos = p_len + jnp.bitwise_and(rows, t - 1)
        visible = jnp.logical_and(jnp.right_shift(p_len + cols, shift) <= jnp.right_shift(qpos, shift), cols < t)
        mla_block(ckvn_ref[0].astype(BF16), krn_ref[0].astype(BF16), visible)

    @pl.when(j > 0)
    def _():
        @pl.when(alive[0] > 0)
        def _():
            sb_blocks(lambda h: kc_ref[0, :, h, :].astype(BF16), lambda h: vc_ref[0, :, h, :].astype(BF16),
                      None, tri_ref[...])

        mla_block(ckvc_ref[0].astype(BF16), krc_ref[0].astype(BF16), None)

    @pl.when(j == nblk)
    def _():
        sbo_ref[0] = jnp.concatenate([acc_scr[h] for h in range(sb_heads)], axis=1)
        o = (olat_scr[...] / l_scr[...]).astype(BF16)
        acc = jnp.dot(o[0:t], wuv_ref[0], preferred_element_type=F32)
        for h in range(1, heads):
            acc = acc + jnp.dot(o[h * t:(h + 1) * t], wuv_ref[h], preferred_element_type=F32)
        mlao_ref[0] = acc


def _dec_attn(qsb, knew, vnew, c_k, c_v, qcat, ckvn, krn, c_ckv, c_kr, wts, tk):
    b, t, sbw = qsb.shape
    p_len, sb_heads, sb_dim = c_k.shape[1:]
    kvr, rdim = c_ckv.shape[2], c_kr.shape[2]
    heads = qcat.shape[2] // LANES
    vw = wts["w_uv_pad"].shape[2]
    tn = LANES
    nblk = p_len // tk
    assert p_len % tk == 0 and t & (t - 1) == 0 and t <= tn <= tk and p_len % CHUNK == 0
    assert sb_heads * sb_dim == sbw
    padn = lambda a: jnp.pad(a, ((0, 0), (0, tn - t), (0, 0)))
    tri = (jnp.arange(tk)[:, None] >= jnp.arange(tk)[None, :]).astype(BF16)
    new = lambda w: pl.BlockSpec((1, tn, w), lambda bi, j: (bi, 0, 0))
    qspec = lambda w: pl.BlockSpec((1, t, w), lambda bi, j: (bi, 0, 0))
    cache = lambda w: pl.BlockSpec((1, tk, w), lambda bi, j: (bi, nblk - jnp.maximum(j, 1), 0))
    cache_sb = pl.BlockSpec((1, tk, sb_heads, sb_dim), lambda bi, j: (bi, nblk - jnp.maximum(j, 1), 0, 0))
    full = lambda a: pl.BlockSpec(a.shape, lambda bi, j: (0,) * a.ndim)
    kern = functools.partial(_dec_kernel, t=t, tn=tn, p_len=p_len, sb_heads=sb_heads, sb_dim=sb_dim,
                             heads=heads, nblk=nblk)
    return pl.pallas_call(
        kern,
        grid=(b, nblk + 1),
        in_specs=[qspec(sbw), new(sbw), new(sbw), cache_sb, cache_sb, qspec(heads * LANES),
                  new(kvr), new(rdim), cache(kvr), cache(rdim),
                  full(tri), full(wts["w_ukt_pad"]), full(wts["w_uv_pad"]), full(wts["place"])],
        out_specs=[qspec(sbw), qspec(vw)],
        out_shape=[jax.ShapeDtypeStruct((b, t, sbw), F32), jax.ShapeDtypeStruct((b, t, vw), F32)],
        scratch_shapes=[pltpu.VMEM((sb_heads, t, sb_dim), BF16), pltpu.VMEM((sb_heads * t, 1), F32),
                        pltpu.VMEM((sb_heads, t, sb_dim), F32),
                        pltpu.VMEM((heads * t, kvr), BF16), pltpu.VMEM((heads * t, rdim), BF16),
                        pltpu.VMEM((heads * t, 1), F32), pltpu.VMEM((heads * t, 1), F32),
                        pltpu.VMEM((heads * t, kvr), F32), pltpu.SMEM((1,), jnp.int32)],
        compiler_params=_cparams(("arbitrary", "arbitrary")),
        name="dec_attn",
    )(qsb, padn(knew), padn(vnew), c_k, c_v, qcat, padn(ckvn), padn(krn), c_ckv, c_kr,
      tri, wts["w_ukt_pad"], wts["w_uv_pad"], wts["place"])


def _topk_rows(s, k, payload=None):
    rows = s.shape[0]
    rid = lax.broadcasted_iota(jnp.int32, s.shape, 0).astype(F32)
    vals, picks = [], []
    for _ in range(k):
        m = jnp.max(s, axis=0, keepdims=True)
        first = jnp.min(jnp.where(s == m, rid, float(rows)), axis=0, keepdims=True)
        hit = rid == first
        vals.append(m)
        if payload is None:
            picks.append(first)
        else:
            picks.append(jnp.max(jnp.where(hit, payload, -1.0), axis=0, keepdims=True))
        s = jnp.where(hit, -jnp.inf, s)
    return jnp.concatenate(vals, axis=0), jnp.concatenate(picks, axis=0)


def _staircase(k):
    return [(a, b) for a in range(k) for b in range(k) if (a + 1) * (b + 1) <= k]


def _out_kernel(x_ref, sbo_ref, mlao_ref, gt1_ref, sc2_ref, sh2_ref, gsb_ref, gmla_ref, gpost_ref, gffn_ref,
                wout_ref, wpq_ref, k1_ref, k2_ref, x1_o, h2_o, idx_o, gate_o, q_scr, *, peer_heads, n_keys):
    sb = _rms(sbo_ref[...], gsb_ref[...])
    ml = _rms(mlao_ref[...], gmla_ref[...])
    merged = jnp.concatenate([sb, ml], axis=1).astype(BF16)
    y = jnp.dot(merged, wout_ref[...], preferred_element_type=F32)
    x1 = x_ref[...] + gt1_ref[0] * _rms(y, gpost_ref[...])
    x1_o[...] = x1
    h2 = _rms(x1, gffn_ref[...]) * (1.0 + sc2_ref[0]) + sh2_ref[0]
    h2_o[...] = h2
    q = jnp.dot(h2.astype(BF16), wpq_ref[...], preferred_element_type=F32)
    half = q.shape[1] // (2 * peer_heads)
    for hd in range(peer_heads):
        for side in range(2):
            o = (2 * hd + side) * half
            q_scr[hd, side] = q[:, o:o + half].astype(BF16)
    pairs = _staircase(PEER_TOPK)
    k1 = k1_ref[...]
    k2 = k2_ref[...]

    def per_head(hd, carry):
        s1 = lax.dot_general(k1, q_scr[hd, 0], NT_DIMS, preferred_element_type=F32)
        s2 = lax.dot_general(k2, q_scr[hd, 1], NT_DIMS, preferred_element_type=F32)
        v1, i1 = _topk_rows(s1, PEER_TOPK)
        v2, i2 = _topk_rows(s2, PEER_TOPK)
        fill = (-len(pairs)) % 8
        tokens = v1.shape[1]
        cs = jnp.concatenate([v1[a:a + 1] + v2[b:b + 1] for a, b in pairs]
                             + [jnp.full((fill, tokens), -jnp.inf, F32)], axis=0)
        ci = jnp.concatenate([i1[a:a + 1] * float(n_keys) + i2[b:b + 1] for a, b in pairs]
                             + [jnp.full((fill, tokens), -1.0, F32)], axis=0)
        ts, ti = _topk_rows(cs, PEER_TOPK, payload=ci)
        ex = jnp.exp(ts - ts[0:1])
        gate_o[hd] = ex / jnp.sum(ex, axis=0, keepdims=True)
        idx_o[hd] = ti.astype(jnp.int32)
        return carry

    lax.fori_loop(0, peer_heads, per_head, 0)


def _out_peer(x2, sbo, mlao, gt1, sc2, sh2, wts, dims, tm, tiles_per_batch):
    n, d = x2.shape
    ph, nk = dims["peer_heads"], dims["n_keys"]
    half = wts["w_pq"].shape[1] // (2 * ph)
    full = lambda a: pl.BlockSpec(a.shape, lambda i: (0,) * a.ndim)
    row = lambda w: pl.BlockSpec((tm, w), lambda i: (i, 0))
    ada = lambda a: _ada_spec(a, tm, tiles_per_batch)
    names = ["g_sb", "g_mla", "g_post", "g_ffn", "w_out", "w_pq", "k1", "k2"]
    return pl.pallas_call(
        functools.partial(_out_kernel, peer_heads=ph, n_keys=nk),
        grid=(n // tm,),
        in_specs=[row(d), row(sbo.shape[1]), row(mlao.shape[1]), ada(gt1), ada(sc2), ada(sh2)]
                 + [full(wts[k]) for k in names],
        out_specs=[row(d), row(d),
                   pl.BlockSpec((ph, PEER_TOPK, tm), lambda i: (0, 0, i)),
                   pl.BlockSpec((ph, PEER_TOPK, tm), lambda i: (0, 0, i))],
        out_shape=[jax.ShapeDtypeStruct((n, d), F32), jax.ShapeDtypeStruct((n, d), F32),
                   jax.ShapeDtypeStruct((ph, PEER_TOPK, n), jnp.int32),
                   jax.ShapeDtypeStruct((ph, PEER_TOPK, n), F32)],
        scratch_shapes=[pltpu.VMEM((ph, 2, tm, half), BF16)],
        compiler_params=_cparams(("arbitrary",)),
        name="out_peer",
    )(x2, sbo, mlao, gt1, sc2, sh2, *[wts[k] for k in names])


def _gelu(x):
    return 0.5 * x * (1.0 + lax.erf(x * (2.0 ** -0.5)))


def _split_bf16(x):
    hi = x.astype(BF16)
    return hi, (x - hi.astype(F32)).astype(BF16)


def _peer_kernel(idx_hbm, slot_hbm, tab_hbm, h2_ref, gate_ref, x1_ref, gt2_ref, gpost_ref, y_o,
                 idx_smem, gate_scr, f_scr, idx_sem, row_sem, *bufs,
                 tt, picks, chunks, grp, nsets, ntiles):
    tile = pl.program_id(0)
    par = tile % 2
    ahead = (nsets - 1) * grp
    n_it = tt // (grp * nsets)

    def idx_copy(tl, half):
        return pltpu.make_async_copy(idx_hbm.at[tl], idx_smem.at[half], idx_sem.at[half])

    def row_copy(e, slot, k):
        return pltpu.make_async_copy(tab_hbm.at[e], bufs[slot].at[:, k, :], row_sem.at[slot])

    def issue(half, t, slot):
        for k in range(picks):
            row_copy(idx_smem[half, k, t], slot, k).start(priority=k % 2)

    def wait_slot(slot):
        pltpu.make_async_copy(slot_hbm.at[0], bufs[slot], row_sem.at[slot]).wait()

    @pl.when(tile == 0)
    def _():
        first = idx_copy(0, 0)
        first.start()
        first.wait()
        for t in range(ahead):
            def one(k, carry, t=t):
                row_copy(idx_smem[0, k, t], t, k).start()
                return carry
            lax.fori_loop(0, picks, one, 0)

    nxt = idx_copy((tile + 1) % ntiles, 1 - par)
    nxt.start()
    gate_scr[...] = gate_ref[...].T

    ones8 = jnp.ones((8, LANES), BF16)
    eye = (lax.broadcasted_iota(jnp.int32, (picks, picks), 0)
           == lax.broadcasted_iota(jnp.int32, (picks, picks), 1)).astype(BF16)

    def compute(slot, t):
        buf = bufs[slot]
        xt = h2_ref[pl.ds(t, 1), :]
        part = buf[0] * xt[:, 0:LANES]
        for c in range(1, chunks):
            part = part + buf[c] * xt[:, c * LANES:(c + 1) * LANES]
        hi, lo = _split_bf16(part)
        s8 = (lax.dot_general(ones8, hi, NT_DIMS, preferred_element_type=F32)
              + lax.dot_general(ones8, lo, NT_DIMS, preferred_element_type=F32))
        a8 = _gelu(s8) * gate_scr[pl.ds(t, 1), :]
        ah, al = _split_bf16(a8)
        reps = picks // 8
        acol = (lax.dot_general(eye, jnp.concatenate([ah] * reps, axis=0), NT_DIMS, preferred_element_type=F32)
                + lax.dot_general(eye, jnp.concatenate([al] * reps, axis=0), NT_DIMS, preferred_element_type=F32))
        outs = [jnp.sum(acol * buf[chunks + c], axis=0, keepdims=True) for c in range(chunks)]
        f_scr[pl.ds(t, 1), :] = jnp.concatenate(outs, axis=1)

    def body(it, carry):
        @pl.when(it == n_it - 1)
        def _():
            nxt.wait()

        for s in range(nsets):
            t0 = (it * nsets + s) * grp
            for j in range(grp):
                wait_slot(s * grp + j)
            for j in range(grp):
                la = t0 + j + ahead
                cross = la >= tt
                issue(jnp.where(cross, 1 - par, par), jnp.where(cross, la - tt, la),
                      ((s + nsets - 1) % nsets) * grp + j)
                compute(s * grp + j, t0 + j)
        return carry

    lax.fori_loop(0, n_it, body, 0)

    @pl.when(tile == ntiles - 1)
    def _():
        for slot in range(ahead):
            wait_slot(slot)

    y_o[...] = x1_ref[...] + gt2_ref[0] * _rms(f_scr[...], gpost_ref[...])


def _peer_ffn(idx, h2, gate, x1, gt2, g_post, table, tt, tiles_per_batch, grp=4, nsets=4):
    n, d = h2.shape
    picks = gate.shape[0]
    chunks = d // LANES
    ntiles = n // tt
    assert table.shape[1:] == (2 * chunks, LANES) and tt % (grp * nsets) == 0 and picks % 8 == 0
    assert table.shape[0] % picks == 0
    idx_tiles = idx.reshape(picks, ntiles, tt).transpose(1, 0, 2)
    slot_view = table.reshape(-1, 2 * chunks, picks, LANES)
    kern = functools.partial(_peer_kernel, tt=tt, picks=picks, chunks=chunks, grp=grp, nsets=nsets,
                             ntiles=ntiles)
    row = lambda w: pl.BlockSpec((tt, w), lambda i: (i, 0))
    hbm = pl.BlockSpec(memory_space=pl.ANY)
    return pl.pallas_call(
        kern,
        grid=(ntiles,),
        in_specs=[hbm, hbm, hbm, row(d), pl.BlockSpec((picks, tt), lambda i: (0, i)), row(d),
                  _ada_spec(gt2, tt, tiles_per_batch), pl.BlockSpec(g_post.shape, lambda i: (0, 0))],
        out_specs=row(d),
        out_shape=jax.ShapeDtypeStruct((n, d), F32),
        scratch_shapes=[pltpu.SMEM((2, picks, tt), jnp.int32),
                        pltpu.VMEM((tt, picks), F32),
                        pltpu.VMEM((tt, d), F32),
                        pltpu.SemaphoreType.DMA((2,)),
                        pltpu.SemaphoreType.DMA((grp * nsets,))]
                       + [pltpu.VMEM((2 * chunks, picks, LANES), F32) for _ in range(grp * nsets)],
        compiler_params=_cparams(("arbitrary",)),
        name="peer_ffn",
    )(idx_tiles, slot_view, table, h2, gate, x1, gt2, g_post)


def _rope_table(pos, rdim, q_scale):
    half = rdim // 2
    inv = ROPE_BASE ** (-jnp.arange(half, dtype=F32) / half)
    ang = pos.astype(F32)[:, None] * inv[None, :]
    cos = jnp.concatenate([jnp.cos(ang), jnp.cos(ang)], axis=1)
    sin = jnp.concatenate([-jnp.sin(ang), jnp.sin(ang)], axis=1)
    t = pos.shape[0]
    z = lambda w: jnp.zeros((t, w), F32)
    cq = jnp.concatenate([jnp.ones((t, 64), F32), cos, z(64 - rdim)], axis=1) * q_scale
    sq = jnp.concatenate([z(64), sin, z(64 - rdim)], axis=1) * q_scale
    ck = jnp.concatenate([cos, z(64 - rdim), cos, z(64 - rdim)], axis=1)
    sk = jnp.concatenate([sin, z(64 - rdim), sin, z(64 - rdim)], axis=1)
    return jnp.concatenate([cq, sq, ck, sk], axis=1)


def _swap_halves(w):
    half = w.shape[-1] // 2
    return jnp.concatenate([w[..., half:], w[..., :half]], axis=-1)


def _prep_weights(w_in, g_q_lat, g_kv_lat, w_uq, w_uk, w_uv, dims):
    d = w_in.shape[0]
    sbw, qr, kvr, rdim = dims["sbw"], dims["qr"], dims["kvr"], dims["rdim"]
    heads, nope, vdim = dims["mla_heads"], dims["nope"], dims["vdim"]
    assert nope == 64 and rdim <= 64 and vdim == 64 and dims["sb_dim"] == 64
    o = 3 * sbw
    w_kr = w_in[:, o + qr + kvr:]
    zk = jnp.zeros((d, 64 - rdim), F32)
    kr_a = jnp.concatenate([w_kr, zk, w_kr, zk], axis=1)
    kr_b = jnp.concatenate([_swap_halves(w_kr), zk, _swap_halves(w_kr), zk], axis=1)
    w_all = jnp.concatenate([w_in[:, :o + qr + kvr], kr_a, kr_b], axis=1).astype(BF16)
    uq = w_uq.reshape(qr, heads, nope + rdim)
    zq = jnp.zeros((qr, heads, LANES - nope - rdim), F32)
    uq_a = jnp.concatenate([uq, zq], axis=2)
    uq_b = jnp.concatenate([jnp.zeros((qr, heads, nope), F32), _swap_halves(uq[..., nope:]), zq], axis=2)
    w_uq2 = jnp.concatenate([uq_a.reshape(qr, -1), uq_b.reshape(qr, -1)], axis=1).astype(BF16)
    uk = jnp.concatenate([w_uk, jnp.zeros((kvr, heads, LANES - nope), F32)], axis=2).reshape(kvr, -1)
    w_ukv = jnp.concatenate([uk, w_uv.reshape(kvr, -1)], axis=1).astype(BF16)
    place = (jnp.arange(rdim)[:, None] + 64 == jnp.arange(LANES)[None, :]).astype(BF16)
    w_ukt_pad = jnp.pad(jnp.transpose(w_uk, (1, 2, 0)), ((0, 0), (0, LANES - nope), (0, 0))).astype(BF16)
    head_cols = (jnp.arange(heads)[:, None, None] == jnp.arange(heads)[None, :, None]) * jnp.ones((1, 1, vdim), F32)
    w_uv_pad = (w_uv[None] * head_cols[:, None]).reshape(heads, kvr, heads * vdim).astype(BF16)
    return dict(w_all=w_all, w_uq2=w_uq2, w_ukv=w_ukv, place=place, w_ukt_pad=w_ukt_pad, w_uv_pad=w_uv_pad,
                g_q=g_q_lat.reshape(1, -1), g_kv=g_kv_lat.reshape(1, -1))


def _layer(x, ada, pos0, past, lw, dims, tq_sb, tq_mla, tk_sb, tk_mla, tk_dec, tm_in, tm_out, tt):
    b, t, d = x.shape
    n = b * t
    x2 = x.reshape(n, d)
    per_token = t < tm_in
    if per_token:
        sh1, sc1, gt1, sh2, sc2, gt2 = [jnp.repeat(a, t, axis=0)[None] for a in ada]
        tiles_in = tiles_out = tiles_tt = 1
    else:
        sh1, sc1, gt1, sh2, sc2, gt2 = [a[:, None, :] for a in ada]
        tiles_in, tiles_out, tiles_tt = t // tm_in, t // tm_out, t // tt
    tm_in, tm_out, tt = min(tm_in, n), min(tm_out, n), min(tt, n)
    heads = dims["mla_heads"]
    q_scale = (dims["nope"] + dims["rdim"]) ** -0.5
    rope_tab = _rope_table(pos0 + jnp.arange(t), dims["rdim"], q_scale)
    rope_tab = jnp.tile(rope_tab, (b, 1)) if per_token else rope_tab
    (sbq, sbk, sbv, sbkb, sbvb, ckvn, kr, qcat, kcat, vmla) = _in_proj(
        x2, sc1, sh1, lw["g_mix_pre"], rope_tab, lw, dims, tm_in, tiles_in)
    sbw, gw, vw = dims["sbw"], heads * LANES, vmla.shape[1]
    r3 = lambda a: a.reshape(b, t, a.shape[-1])
    if past is None:
        sbo = _sb_attn(r3(sbq), r3(sbkb), r3(sbvb), tq_sb, tk_sb, 0)
        mlao = _mla_attn(r3(qcat), r3(kcat), r3(vmla), tq_mla, tk_mla, 0, t)
    else:
        c_k, c_v, c_ckv, c_kr = past
        p_len = c_k.shape[1]
        sbo, mlao = _dec_attn(r3(sbq), r3(sbkb), r3(sbvb), c_k, c_v, r3(qcat), r3(ckvn), r3(kr), c_ckv, c_kr,
                              lw, tk_dec)
    x1, h2, idx, gate = _out_peer(x2, sbo.reshape(n, sbw), mlao.reshape(n, vw), gt1, sc2, sh2, lw, dims,
                                  tm_out, tiles_out)
    picks = dims["peer_heads"] * PEER_TOPK
    y = _peer_ffn(idx.reshape(picks, n), h2, gate.reshape(picks, n), x1, gt2, lw["g_post_ffn"], lw["table"],
                  tt, tiles_tt)
    new = (sbk.reshape(b, t, dims["sb_heads"], dims["sb_dim"]), sbv.reshape(b, t, dims["sb_heads"], dims["sb_dim"]),
           ckvn.reshape(b, t, -1), kr.reshape(b, t, -1))
    return y.reshape(b, t, d), new


def kernel(x_prompt, x_sample, c_prompt, c_sample, cache_sb_k, cache_sb_v, cache_mla_ckv, cache_mla_krope,
           w_ada, b_ada, g_mix_pre, g_mix_post, g_ffn_pre, g_ffn_post, w_in, g_q_lat, g_kv_lat, w_uq, w_uk,
           w_uv, g_sb_out, g_mla_out, w_out, w_peer_q, sub_keys_1, sub_keys_2, u_experts, v_experts):
    depth = w_ada.shape[0]
    d = x_prompt.shape[-1]
    sb_heads, sb_dim = cache_sb_k.shape[-2:]
    kvr, mla_heads, nope = w_uk.shape[1:]
    dims = dict(sb_heads=sb_heads, sb_dim=sb_dim, sbw=sb_heads * sb_dim, qr=w_uq.shape[1], kvr=kvr,
                rdim=cache_mla_krope.shape[-1], mla_heads=mla_heads, nope=nope, vdim=w_uv.shape[-1],
                n_keys=sub_keys_1.shape[1], peer_heads=w_peer_q.shape[2] // (2 * sub_keys_1.shape[2]))
    bp, bs = c_prompt.shape[0], c_sample.shape[0]
    pad = (-(bp + bs)) % 8
    y_p, y_s = x_prompt, x_sample
    outs_p, outs_s = [], []
    for l in range(depth):
        c_all = jnp.concatenate([c_prompt, c_sample, jnp.zeros((pad, d), F32)], axis=0)
        ada = _ada(c_all, w_ada[l], b_ada[l])
        ada_p = jnp.split(ada[:bp], 6, axis=1)
        ada_s = jnp.split(ada[bp:bp + bs], 6, axis=1)
        lw = _prep_weights(w_in[l], g_q_lat[l], g_kv_lat[l], w_uq[l], w_uk[l], w_uv[l], dims)
        e = u_experts.shape[1]
        lw.update(
            g_mix_pre=g_mix_pre[l].reshape(1, d), g_post=g_mix_post[l].reshape(1, d),
            g_ffn=g_ffn_pre[l].reshape(1, d), g_post_ffn=g_ffn_post[l].reshape(1, d),
            g_sb=g_sb_out[l].reshape(1, -1), g_mla=g_mla_out[l].reshape(1, -1),
            w_out=w_out[l].astype(BF16), w_pq=w_peer_q[l].astype(BF16),
            k1=sub_keys_1[l].astype(BF16), k2=sub_keys_2[l].astype(BF16),
            table=jnp.concatenate([u_experts[l].reshape(e, d // LANES, LANES),
                                   v_experts[l].reshape(e, d // LANES, LANES)], axis=1))
        tiles = dict(tq_sb=256, tq_mla=1024, tk_sb=256, tk_mla=1024, tk_dec=512, tm_in=512, tm_out=256, tt=256)
        y_p, new_p = _layer(y_p, ada_p, 0, None, lw, dims, **tiles)
        past = (cache_sb_k[l], cache_sb_v[l], cache_mla_ckv[l], cache_mla_krope[l])
        y_s, new_s = _layer(y_s, ada_s, cache_sb_k.shape[2], past, lw, dims, **tiles)
        outs_p.append(new_p)
        outs_s.append(new_s)
    stack = lambda outs, k: jnp.stack([o[k] for o in outs])
    return (y_p, y_s, stack(outs_p, 0), stack(outs_p, 1), stack(outs_p, 2), stack(outs_p, 3),
            stack(outs_s, 0), stack(outs_s, 1), stack(outs_s, 2), stack(outs_s, 3))
```

```python
import functools

import jax
import jax.numpy as jnp
import numpy as np
from jax import lax
from jax.experimental import pallas as pl
from jax.experimental.pallas import tpu as pltpu

F32 = jnp.float32
BF16 = jnp.bfloat16

CHUNK = 64
ROPE_BASE = 10000.0
NORM_EPS = 1e-6
PEER_TOPK = 16
LANES = 128
VMEM_LIMIT = 56 * 1024 * 1024
SB_LOG_FLOOR = -88.0
NT_DIMS = (((1,), (1,)), ((), ()))


def _rms(x, g):
    return x * lax.rsqrt(jnp.mean(x * x, axis=-1, keepdims=True) + NORM_EPS) * g


def _cparams(sem):
    return pltpu.CompilerParams(dimension_semantics=sem, vmem_limit_bytes=VMEM_LIMIT)


def _ada_kernel(c_ref, w_ref, b_ref, o_ref):
    c = c_ref[...]
    s = c * jax.nn.sigmoid(c)
    o_ref[...] = jnp.dot(s.astype(BF16), w_ref[...].astype(BF16), preferred_element_type=F32) + b_ref[...]


def _ada(c_all, w_ada, b_ada):
    rows, d = c_all.shape
    width = w_ada.shape[1]
    return pl.pallas_call(
        _ada_kernel,
        grid=(width // d,),
        in_specs=[pl.BlockSpec((rows, d), lambda j: (0, 0)),
                  pl.BlockSpec((d, d), lambda j: (0, j)),
                  pl.BlockSpec((1, d), lambda j: (0, j))],
        out_specs=pl.BlockSpec((rows, d), lambda j: (0, j)),
        out_shape=jax.ShapeDtypeStruct((rows, width), F32),
        compiler_params=_cparams(("arbitrary",)),
        name="ada",
    )(c_all, w_ada, b_ada.reshape(1, width))


def _in_kernel(x_ref, sc_ref, sh_ref, g_ref, rope_ref, wall_ref, gq_ref, gkv_ref, wuq_ref, wukv_ref,
               sbq_o, sbk_o, sbv_o, sbkb_o, sbvb_o, ckv_o, kr_o, qcat_o, kcat_o, vmla_o,
               *, sbw, qr, kvr, rdim, heads, sb_scale, pos_minor):
    x = x_ref[...]
    h = _rms(x, g_ref[...]) * (1.0 + sc_ref[0]) + sh_ref[0]
    proj = jnp.dot(h.astype(BF16), wall_ref[...], preferred_element_type=F32)
    o = 0
    sbq_o[...] = (proj[:, o:o + sbw] * sb_scale).astype(BF16); o += sbw
    k = proj[:, o:o + sbw]; o += sbw
    v = proj[:, o:o + sbw]; o += sbw
    if pos_minor:
        sbk_o[0] = k.T
        sbv_o[0] = v.T
    else:
        sbk_o[...] = k
        sbv_o[...] = v
    sbkb_o[...] = k.astype(BF16)
    sbvb_o[...] = v.astype(BF16)
    qlat = proj[:, o:o + qr]; o += qr
    ckv = proj[:, o:o + kvr]; o += kvr
    kr_a = proj[:, o:o + LANES]; o += LANES
    kr_b = proj[:, o:o + LANES]
    rt = rope_ref[...]
    cq, sq = rt[:, 0:LANES], rt[:, LANES:2 * LANES]
    ck, sk = rt[:, 2 * LANES:3 * LANES], rt[:, 3 * LANES:4 * LANES]
    kr128 = kr_a * ck + kr_b * sk
    if pos_minor:
        kr_o[0] = kr128.T[0:rdim]
    else:
        kr_o[...] = kr128[:, 0:rdim]
    qn = _rms(qlat, gq_ref[...]).astype(BF16)
    q2 = jnp.dot(qn, wuq_ref[...], preferred_element_type=F32)
    gw = heads * LANES
    cq8 = jnp.concatenate([cq] * heads, axis=1)
    sq8 = jnp.concatenate([sq] * heads, axis=1)
    qcat_o[...] = (q2[:, 0:gw] * cq8 + q2[:, gw:2 * gw] * sq8).astype(BF16)
    ckvn = _rms(ckv, gkv_ref[...])
    ckv_o[...] = ckvn
    kv = jnp.dot(ckvn.astype(BF16), wukv_ref[...], preferred_element_type=F32)
    lane = lax.broadcasted_iota(jnp.int32, kr128.shape, 1)
    kr_hi = jnp.where(lane >= 64, kr128, 0.0)
    kcat_o[...] = (kv[:, 0:gw] + jnp.concatenate([kr_hi] * heads, axis=1)).astype(BF16)
    vmla_o[...] = kv[:, gw:].astype(BF16)


def _ada_spec(arr, tm, tiles_per_batch):
    d = arr.shape[-1]
    if arr.shape[1] == 1:
        return pl.BlockSpec((1, 1, d), lambda i: (i // tiles_per_batch, 0, 0))
    return pl.BlockSpec((1, tm, d), lambda i: (0, i, 0))


def _in_proj(x2, sc, sh, g_pre, rope_tab, wts, dims, tm, tiles_per_batch):
    n, d = x2.shape
    sbw, qr, kvr, rdim, heads = dims["sbw"], dims["qr"], dims["kvr"], dims["rdim"], dims["mla_heads"]
    gw = heads * LANES
    vw = wts["w_ukv"].shape[1] - gw
    full = lambda a: pl.BlockSpec(a.shape, lambda i: (0,) * a.ndim)
    row = lambda w: pl.BlockSpec((tm, w), lambda i: (i, 0))
    pos_minor = sc.shape[1] == 1
    kern = functools.partial(_in_kernel, sbw=sbw, qr=qr, kvr=kvr, rdim=rdim, heads=heads,
                             sb_scale=dims["sb_dim"] ** -0.5, pos_minor=pos_minor)
    outs = [(sbw, BF16), (sbw, F32), (sbw, F32), (sbw, BF16), (sbw, BF16), (kvr, F32), (rdim, F32),
            (gw, BF16), (gw, BF16), (vw, BF16)]
    specs = [row(w) for w, _ in outs]
    shapes = [jax.ShapeDtypeStruct((n, w), dt) for w, dt in outs]
    if pos_minor:
        nb = n // (tm * tiles_per_batch)
        for slot in (1, 2, 6):
            w = outs[slot][0]
            specs[slot] = pl.BlockSpec((1, w, tm), lambda i: (i // tiles_per_batch, 0, i % tiles_per_batch))
            shapes[slot] = jax.ShapeDtypeStruct((nb, w, tm * tiles_per_batch), F32)
    return pl.pallas_call(
        kern,
        grid=(n // tm,),
        in_specs=[row(d), _ada_spec(sc, tm, tiles_per_batch), _ada_spec(sh, tm, tiles_per_batch),
                  full(g_pre), pl.BlockSpec((tm, 4 * LANES), lambda i: (i % (rope_tab.shape[0] // tm), 0)),
                  full(wts["w_all"]), full(wts["g_q"]), full(wts["g_kv"]),
                  full(wts["w_uq2"]), full(wts["w_ukv"])],
        out_specs=specs,
        out_shape=shapes,
        compiler_params=_cparams(("arbitrary",)),
        name="in_proj",
    )(x2, sc, sh, g_pre, rope_tab, wts["w_all"], wts["g_q"], wts["g_kv"], wts["w_uq2"], wts["w_ukv"])


def _sb_block(qm, kb, vb, c, acc, mask, tri):
    z = lax.dot_general(qm, kb, NT_DIMS, preferred_element_type=F32)
    sp = jnp.maximum(z, 0.0) + jnp.log(1.0 + jnp.exp(-jnp.abs(z)))
    lk = -sp if mask is None else jnp.where(mask, -sp, 0.0)
    hi = lk.astype(BF16)
    lo = (lk - hi.astype(F32)).astype(BF16)
    incl = jnp.dot(hi, tri, preferred_element_type=F32) + jnp.dot(lo, tri, preferred_element_type=F32)
    a = jnp.exp(z + incl + c)
    if mask is not None:
        a = jnp.where(mask, a, 0.0)
    acc = acc + jnp.dot(a.astype(BF16), vb, preferred_element_type=F32)
    return c + incl[:, 0:1], acc


def _sb_kernel(q_ref, k_ref, v_ref, o_ref, qm_scr, c_scr, acc_scr, *, tq, tk, q_off, pairs):
    i = pl.program_id(1)
    q_start = q_off + i * tq
    jm = q_start // tk
    r = lax.broadcasted_iota(jnp.int32, (tk, tk), 0)
    cc = lax.broadcasted_iota(jnp.int32, (tk, tk), 1)
    tri = (r >= cc).astype(BF16)
    qpos = q_start + lax.broadcasted_iota(jnp.int32, (tq, tk), 0)
    kpos = jm * tk + lax.broadcasted_iota(jnp.int32, (tq, tk), 1)
    causal = jnp.concatenate([kpos < qpos] * 2, axis=0)
    lane = lax.broadcasted_iota(jnp.int32, (tq, LANES), 1)
    for p in range(pairs):
        qp = q_ref[0, :, p * LANES:(p + 1) * LANES]
        zero = jnp.zeros_like(qp)
        qm_scr[p] = jnp.concatenate([jnp.where(lane < 64, qp, zero), jnp.where(lane >= 64, qp, zero)], axis=0)
    c_scr[...] = jnp.zeros_like(c_scr)
    acc_scr[...] = jnp.zeros_like(acc_scr)

    def block(j, mask):
        ks = pl.multiple_of(j * tk, tk)
        worst = None
        for p in range(pairs):
            sl = slice(p * LANES, (p + 1) * LANES)
            c, acc = _sb_block(qm_scr[p], k_ref[0, pl.ds(ks, tk), sl], v_ref[0, pl.ds(ks, tk), sl],
                               c_scr[p], acc_scr[p], mask, tri)
            c_scr[p] = c
            acc_scr[p] = acc
            worst = c if worst is None else jnp.maximum(worst, c)
        return jnp.max(worst) > SB_LOG_FLOOR

    def body(s):
        j, _ = s
        alive = block(j, None)
        return j - 1, jnp.logical_and(j > 0, alive).astype(jnp.int32)

    go0 = jnp.logical_and(jm > 0, block(jm, causal)).astype(jnp.int32)
    lax.while_loop(lambda s: s[1] > 0, body, (jm - 1, go0))
    outs = [jnp.where(lane < 64, acc_scr[p, 0:tq], acc_scr[p, tq:2 * tq]) for p in range(pairs)]
    o_ref[0] = jnp.concatenate(outs, axis=1)


def _sb_attn(q, k, v, tq, tk, q_off):
    b, t_q, w = q.shape
    t_k = k.shape[1]
    assert tk % tq == 0 and q_off % tk == 0 and q_off + t_q <= t_k
    kern = functools.partial(_sb_kernel, tq=tq, tk=tk, q_off=q_off, pairs=w // LANES)
    return pl.pallas_call(
        kern,
        grid=(b, t_q // tq),
        in_specs=[pl.BlockSpec((1, tq, w), lambda bi, i: (bi, i, 0)),
                  pl.BlockSpec((1, t_k, w), lambda bi, i: (bi, 0, 0)),
                  pl.BlockSpec((1, t_k, w), lambda bi, i: (bi, 0, 0))],
        out_specs=pl.BlockSpec((1, tq, w), lambda bi, i: (bi, i, 0)),
        out_shape=jax.ShapeDtypeStruct((b, t_q, w), F32),
        scratch_shapes=[pltpu.VMEM((w // LANES, 2 * tq, LANES), BF16),
                        pltpu.VMEM((w // LANES, 2 * tq, 1), F32),
                        pltpu.VMEM((w // LANES, 2 * tq, LANES), F32)],
        compiler_params=_cparams(("arbitrary", "arbitrary")),
        name="sb_attn",
    )(q, k, v)


def _mla_kernel(q_ref, k_ref, v_ref, o_ref, *, tq, tk, q_off, t_valid, heads):
    i = pl.program_id(1)
    q_start = q_off + i * tq
    jm = q_start // tk
    qpos = q_start + lax.broadcasted_iota(jnp.int32, (tq, tk), 0)
    kpos = jm * tk + lax.broadcasted_iota(jnp.int32, (tq, tk), 1)
    shift = CHUNK.bit_length() - 1
    visible = jnp.logical_and(jnp.right_shift(kpos, shift) <= jnp.right_shift(qpos, shift), kpos < t_valid)
    lane = lax.broadcasted_iota(jnp.int32, (tq, LANES), 1)
    outs = []
    for h in range(heads):
        sl = slice(h * LANES, (h + 1) * LANES)
        vsl = slice((h // 2) * LANES, (h // 2 + 1) * LANES)
        qh = q_ref[0, :, sl]

        def blk(j, m, l, acc, mask):
            ks = pl.multiple_of(j * tk, tk)
            s = lax.dot_general(qh, k_ref[0, pl.ds(ks, tk), sl], NT_DIMS, preferred_element_type=F32)
            if mask is not None:
                s = jnp.where(mask, s, -jnp.inf)
            m_new = jnp.maximum(m, jnp.max(s, axis=1, keepdims=True))
            alpha = jnp.exp(m - m_new)
            p = jnp.exp(s - m_new)
            l = alpha * l + jnp.sum(p, axis=1, keepdims=True)
            acc = alpha * acc + jnp.dot(p.astype(BF16), v_ref[0, pl.ds(ks, tk), vsl],
                                        preferred_element_type=F32)
            return m_new, l, acc

        m, l, acc = blk(jm, jnp.full((tq, 1), -jnp.inf, F32), jnp.zeros((tq, 1), F32),
                        jnp.zeros((tq, LANES), F32), visible)
        m, l, acc = lax.fori_loop(0, jm, lambda j, s: blk(j, *s, None), (m, l, acc))
        outs.append(acc / l)
    merged = [jnp.where(lane < 64, outs[2 * p], outs[2 * p + 1]) for p in range(heads // 2)]
    o_ref[0] = jnp.concatenate(merged, axis=1)


def _mla_attn(q, k, v, tq, tk, q_off, t_valid):
    b, t_q, gw = q.shape
    t_k = k.shape[1]
    vw = v.shape[2]
    assert tk % tq == 0 and q_off % tk == 0 and q_off + t_q <= t_k and tk % CHUNK == 0
    kern = functools.partial(_mla_kernel, tq=tq, tk=tk, q_off=q_off, t_valid=t_valid, heads=gw // LANES)
    return pl.pallas_call(
        kern,
        grid=(b, t_q // tq),
        in_specs=[pl.BlockSpec((1, tq, gw), lambda bi, i: (bi, i, 0)),
                  pl.BlockSpec((1, t_k, gw), lambda bi, i: (bi, 0, 0)),
                  pl.BlockSpec((1, t_k, vw), lambda bi, i: (bi, 0, 0))],
        out_specs=pl.BlockSpec((1, tq, vw), lambda bi, i: (bi, i, 0)),
        out_shape=jax.ShapeDtypeStruct((b, t_q, vw), F32),
        compiler_params=_cparams(("arbitrary", "arbitrary")),
        name="mla_attn",
    )(q, k, v)


def _dec_kernel(qsb_ref, knew_ref, vnew_ref, kc_ref, vc_ref, qcat_ref, ckvn_ref, krn_ref, ckvc_ref, krc_ref,
                tri_ref, wukt_ref, wuv_ref, place_ref, sbo_ref, mlao_ref,
                qh_scr, c_scr, acc_scr, qall_scr, qr_scr, m_scr, l_scr, olat_scr, alive,
                *, t, tn, p_len, sb_heads, sb_dim, heads, nblk):
    j = pl.program_id(1)
    hrows = sb_heads * t

    def sb_blocks(score, pv, mask, tri):
        z = jnp.concatenate([score(h) for h in range(sb_heads)], axis=0)
        sp = jnp.maximum(z, 0.0) + jnp.log(1.0 + jnp.exp(-jnp.abs(z)))
        lk = -sp if mask is None else jnp.where(mask, -sp, 0.0)
        hi, lo = _split_bf16(lk)
        both = jnp.dot(jnp.concatenate([hi, lo], axis=0), tri, preferred_element_type=F32)
        incl = both[0:hrows] + both[hrows:2 * hrows]
        a = jnp.exp(z + incl + c_scr[...])
        if mask is not None:
            a = jnp.where(mask, a, 0.0)
        a = a.astype(BF16)
        for h in range(sb_heads):
            acc_scr[h] = acc_scr[h] + pv(h, a[h * t:(h + 1) * t])
        c = c_scr[...] + incl[:, 0:1]
        c_scr[...] = c
        alive[0] = (jnp.max(c) > SB_LOG_FLOOR).astype(jnp.int32)

    def mla_block(ckv_b, s_rope, mask):
        s = lax.dot_general(qall_scr[...], ckv_b, NT_DIMS, preferred_element_type=F32) + s_rope
        if mask is not None:
            s = jnp.where(mask, s, -jnp.inf)
        m_old = m_scr[...]
        m_new = jnp.maximum(m_old, jnp.max(s, axis=1, keepdims=True))
        alpha = jnp.exp(m_old - m_new)
        p = jnp.exp(s - m_new)
        l_scr[...] = alpha * l_scr[...] + jnp.sum(p, axis=1, keepdims=True)
        olat_scr[...] = alpha * olat_scr[...] + jnp.dot(p.astype(BF16), ckv_b, preferred_element_type=F32)
        m_scr[...] = m_new

    pair = lambda p: slice(p * LANES, (p + 1) * LANES)

    head = lambda h: slice(h * sb_dim, (h + 1) * sb_dim)

    @pl.when(j == 0)
    def _():
        for h in range(sb_heads):
            qh_scr[h] = qsb_ref[0, :, head(h)]
        c_scr[...] = jnp.zeros_like(c_scr)
        acc_scr[...] = jnp.zeros_like(acc_scr)
        for h in range(heads):
            qh = qcat_ref[0, :, pair(h)]
            qall_scr[h * t:(h + 1) * t, :] = jnp.dot(qh, wukt_ref[h], preferred_element_type=F32).astype(BF16)
            qr_scr[h * t:(h + 1) * t, :] = lax.dot_general(qh, place_ref[...], NT_DIMS,
                                                           preferred_element_type=F32).astype(BF16)
        m_scr[...] = jnp.full_like(m_scr, -jnp.inf)
        l_scr[...] = jnp.zeros_like(l_scr)
        olat_scr[...] = jnp.zeros_like(olat_scr)
        row2 = lax.broadcasted_iota(jnp.int32, (hrows, tn), 0)
        col2 = lax.broadcasted_iota(jnp.int32, (hrows, tn), 1)
        causal = col2 < jnp.bitwise_and(row2, t - 1)
        sb_blocks(lambda h: lax.dot_general(qh_scr[h], knew_ref[0, :, head(h)], NT_DIMS,
                                            preferred_element_type=F32),
                  lambda h, a: jnp.dot(a, vnew_ref[0, :, head(h)], preferred_element_type=F32),
                  causal, tri_ref[0:tn, 0:tn])
        rows = lax.broadcasted_iota(jnp.int32, (heads * t, tn), 0)
        cols = lax.broadcasted_iota(jnp.int32, (heads * t, tn), 1)
        shift = CHUNK.bit_length() - 1
        qpos = p_len + jnp.bitwise_and(rows, t - 1)
        visible = jnp.logical_and(jnp.right_shift(p_len + cols, shift) <= jnp.right_shift(qpos, shift), cols < t)
        mla_block(ckvn_ref[0].astype(BF16),
                  lax.dot_general(qr_scr[...], krn_ref[0].astype(BF16), NT_DIMS, preferred_element_type=F32),
                  visible)

    @pl.when(j > 0)
    def _():
        @pl.when(alive[0] > 0)
        def _():
            sb_blocks(lambda h: jnp.dot(qh_scr[h], kc_ref[0, h].astype(BF16), preferred_element_type=F32),
                      lambda h, a: lax.dot_general(a, vc_ref[0, h].astype(BF16), NT_DIMS,
                                                   preferred_element_type=F32),
                      None, tri_ref[...])

        mla_block(ckvc_ref[0].astype(BF16),
                  jnp.dot(qr_scr[...], krc_ref[0].astype(BF16), preferred_element_type=F32), None)

    @pl.when(j == nblk)
    def _():
        sbo_ref[0] = jnp.concatenate([acc_scr[h] for h in range(sb_heads)], axis=1)
        o = (olat_scr[...] / l_scr[...]).astype(BF16)
        acc = jnp.dot(o[0:t], wuv_ref[0], preferred_element_type=F32)
        for h in range(1, heads):
            acc = acc + jnp.dot(o[h * t:(h + 1) * t], wuv_ref[h], preferred_element_type=F32)
        mlao_ref[0] = acc


def _dec_attn(qsb, knew, vnew, c_k, c_v, qcat, ckvn, krn, c_ckv, c_kr, wts, tk):
    b, t, sbw = qsb.shape
    sb_heads, sb_dim, p_len = c_k.shape[1:]
    kvr, rdim = c_ckv.shape[2], c_kr.shape[1]
    heads = qcat.shape[2] // LANES
    vw = wts["w_uv_pad"].shape[2]
    tn = LANES
    nblk = p_len // tk
    assert p_len % tk == 0 and t & (t - 1) == 0 and t <= tn <= tk and p_len % CHUNK == 0
    assert sb_heads * sb_dim == sbw
    padn = lambda a: jnp.pad(a, ((0, 0), (0, tn - t), (0, 0)))
    tri = (jnp.arange(tk)[:, None] >= jnp.arange(tk)[None, :]).astype(BF16)
    new = lambda w: pl.BlockSpec((1, tn, w), lambda bi, j: (bi, 0, 0))
    qspec = lambda w: pl.BlockSpec((1, t, w), lambda bi, j: (bi, 0, 0))
    cache = lambda w: pl.BlockSpec((1, tk, w), lambda bi, j: (bi, nblk - jnp.maximum(j, 1), 0))
    cache_sb = pl.BlockSpec((1, sb_heads, sb_dim, tk), lambda bi, j: (bi, 0, 0, nblk - jnp.maximum(j, 1)))
    cache_kr = pl.BlockSpec((1, rdim, tk), lambda bi, j: (bi, 0, nblk - jnp.maximum(j, 1)))
    full = lambda a: pl.BlockSpec(a.shape, lambda bi, j: (0,) * a.ndim)
    kern = functools.partial(_dec_kernel, t=t, tn=tn, p_len=p_len, sb_heads=sb_heads, sb_dim=sb_dim,
                             heads=heads, nblk=nblk)
    return pl.pallas_call(
        kern,
        grid=(b, nblk + 1),
        in_specs=[qspec(sbw), new(sbw), new(sbw), cache_sb, cache_sb, qspec(heads * LANES),
                  new(kvr), new(rdim), cache(kvr), cache_kr,
                  full(tri), full(wts["w_ukt_pad"]), full(wts["w_uv_pad"]), full(wts["place"])],
        out_specs=[qspec(sbw), qspec(vw)],
        out_shape=[jax.ShapeDtypeStruct((b, t, sbw), F32), jax.ShapeDtypeStruct((b, t, vw), F32)],
        scratch_shapes=[pltpu.VMEM((sb_heads, t, sb_dim), BF16), pltpu.VMEM((sb_heads * t, 1), F32),
                        pltpu.VMEM((sb_heads, t, sb_dim), F32),
                        pltpu.VMEM((heads * t, kvr), BF16), pltpu.VMEM((heads * t, rdim), BF16),
                        pltpu.VMEM((heads * t, 1), F32), pltpu.VMEM((heads * t, 1), F32),
                        pltpu.VMEM((heads * t, kvr), F32), pltpu.SMEM((1,), jnp.int32)],
        compiler_params=_cparams(("arbitrary", "arbitrary")),
        name="dec_attn",
    )(qsb, padn(knew), padn(vnew), c_k, c_v, qcat, padn(ckvn), padn(krn), c_ckv, c_kr,
      tri, wts["w_ukt_pad"], wts["w_uv_pad"], wts["place"])


def _topk_rows(s, k, payload=None):
    rows = s.shape[0]
    rid = lax.broadcasted_iota(jnp.int32, s.shape, 0).astype(F32)
    vals, picks = [], []
    for _ in range(k):
        m = jnp.max(s, axis=0, keepdims=True)
        first = jnp.min(jnp.where(s == m, rid, float(rows)), axis=0, keepdims=True)
        hit = rid == first
        vals.append(m)
        if payload is None:
            picks.append(first)
        else:
            picks.append(jnp.max(jnp.where(hit, payload, -1.0), axis=0, keepdims=True))
        s = jnp.where(hit, -jnp.inf, s)
    return jnp.concatenate(vals, axis=0), jnp.concatenate(picks, axis=0)


def _staircase(k):
    return [(a, b) for a in range(k) for b in range(k) if (a + 1) * (b + 1) <= k]


def _out_kernel(x_ref, sbo_ref, mlao_ref, gt1_ref, sc2_ref, sh2_ref, gsb_ref, gmla_ref, gpost_ref, gffn_ref,
                wout_ref, wpq_ref, k1_ref, k2_ref, x1_o, h2_o, idx_o, gate_o, q_scr, *, peer_heads, n_keys):
    sb = _rms(sbo_ref[...], gsb_ref[...])
    ml = _rms(mlao_ref[...], gmla_ref[...])
    merged = jnp.concatenate([sb, ml], axis=1).astype(BF16)
    y = jnp.dot(merged, wout_ref[...], preferred_element_type=F32)
    x1 = x_ref[...] + gt1_ref[0] * _rms(y, gpost_ref[...])
    x1_o[...] = x1
    h2 = _rms(x1, gffn_ref[...]) * (1.0 + sc2_ref[0]) + sh2_ref[0]
    h2_o[...] = h2
    q = jnp.dot(h2.astype(BF16), wpq_ref[...], preferred_element_type=F32)
    half = q.shape[1] // (2 * peer_heads)
    for hd in range(peer_heads):
        for side in range(2):
            o = (2 * hd + side) * half
            q_scr[hd, side] = q[:, o:o + half].astype(BF16)
    pairs = _staircase(PEER_TOPK)
    k1 = k1_ref[...]
    k2 = k2_ref[...]

    def per_head(hd, carry):
        s1 = lax.dot_general(k1, q_scr[hd, 0], NT_DIMS, preferred_element_type=F32)
        s2 = lax.dot_general(k2, q_scr[hd, 1], NT_DIMS, preferred_element_type=F32)
        v1, i1 = _topk_rows(s1, PEER_TOPK)
        v2, i2 = _topk_rows(s2, PEER_TOPK)
        fill = (-len(pairs)) % 8
        tokens = v1.shape[1]
        cs = jnp.concatenate([v1[a:a + 1] + v2[b:b + 1] for a, b in pairs]
                             + [jnp.full((fill, tokens), -jnp.inf, F32)], axis=0)
        ci = jnp.concatenate([i1[a:a + 1] * float(n_keys) + i2[b:b + 1] for a, b in pairs]
                             + [jnp.full((fill, tokens), -1.0, F32)], axis=0)
        ts, ti = _topk_rows(cs, PEER_TOPK, payload=ci)
        ex = jnp.exp(ts - ts[0:1])
        gate_o[hd] = ex / jnp.sum(ex, axis=0, keepdims=True)
        idx_o[hd] = ti.astype(jnp.int32)
        return carry

    lax.fori_loop(0, peer_heads, per_head, 0)


def _out_peer(x2, sbo, mlao, gt1, sc2, sh2, wts, dims, tm, tiles_per_batch):
    n, d = x2.shape
    ph, nk = dims["peer_heads"], dims["n_keys"]
    half = wts["w_pq"].shape[1] // (2 * ph)
    full = lambda a: pl.BlockSpec(a.shape, lambda i: (0,) * a.ndim)
    row = lambda w: pl.BlockSpec((tm, w), lambda i: (i, 0))
    ada = lambda a: _ada_spec(a, tm, tiles_per_batch)
    names = ["g_sb", "g_mla", "g_post", "g_ffn", "w_out", "w_pq", "k1", "k2"]
    return pl.pallas_call(
        functools.partial(_out_kernel, peer_heads=ph, n_keys=nk),
        grid=(n // tm,),
        in_specs=[row(d), row(sbo.shape[1]), row(mlao.shape[1]), ada(gt1), ada(sc2), ada(sh2)]
                 + [full(wts[k]) for k in names],
        out_specs=[row(d), row(d),
                   pl.BlockSpec((ph, PEER_TOPK, tm), lambda i: (0, 0, i)),
                   pl.BlockSpec((ph, PEER_TOPK, tm), lambda i: (0, 0, i))],
        out_shape=[jax.ShapeDtypeStruct((n, d), F32), jax.ShapeDtypeStruct((n, d), F32),
                   jax.ShapeDtypeStruct((ph, PEER_TOPK, n), jnp.int32),
                   jax.ShapeDtypeStruct((ph, PEER_TOPK, n), F32)],
        scratch_shapes=[pltpu.VMEM((ph, 2, tm, half), BF16)],
        compiler_params=_cparams(("arbitrary",)),
        name="out_peer",
    )(x2, sbo, mlao, gt1, sc2, sh2, *[wts[k] for k in names])


def _gelu(x):
    return 0.5 * x * (1.0 + lax.erf(x * (2.0 ** -0.5)))


def _split_bf16(x):
    hi = x.astype(BF16)
    return hi, (x - hi.astype(F32)).astype(BF16)


def _peer_kernel(idx_hbm, slot_hbm, tab_hbm, h2_ref, gate_ref, x1_ref, gt2_ref, gpost_ref, y_o,
                 idx_smem, gate_scr, f_scr, idx_sem, row_sem, *bufs,
                 tt, picks, chunks, grp, nsets, ntiles):
    tile = pl.program_id(0)
    par = tile % 2
    ahead = (nsets - 1) * grp
    n_it = tt // (grp * nsets)

    def idx_copy(tl, half):
        return pltpu.make_async_copy(idx_hbm.at[tl], idx_smem.at[half], idx_sem.at[half])

    def row_copy(e, slot, k):
        return pltpu.make_async_copy(tab_hbm.at[e], bufs[slot].at[:, k, :], row_sem.at[slot])

    def issue(half, t, slot):
        for k in range(picks):
            row_copy(idx_smem[half, t, k], slot, k).start(priority=k % 2)

    def wait_slot(slot):
        pltpu.make_async_copy(slot_hbm.at[0], bufs[slot], row_sem.at[slot]).wait()

    @pl.when(tile == 0)
    def _():
        first = idx_copy(0, 0)
        first.start()
        first.wait()
        for t in range(ahead):
            def one(k, carry, t=t):
                row_copy(idx_smem[0, t, k], t, k).start()
                return carry
            lax.fori_loop(0, picks, one, 0)

    nxt = idx_copy((tile + 1) % ntiles, 1 - par)
    nxt.start()
    gate_scr[...] = gate_ref[...].T

    ones8 = jnp.ones((8, LANES), BF16)
    eye = (lax.broadcasted_iota(jnp.int32, (picks, picks), 0)
           == lax.broadcasted_iota(jnp.int32, (picks, picks), 1)).astype(BF16)

    def compute(slot, t):
        buf = bufs[slot]
        xt = h2_ref[pl.ds(t, 1), :]
        part = buf[0] * xt[:, 0:LANES]
        for c in range(1, chunks):
            part = part + buf[c] * xt[:, c * LANES:(c + 1) * LANES]
        hi, lo = _split_bf16(part)
        s8 = (lax.dot_general(ones8, hi, NT_DIMS, preferred_element_type=F32)
              + lax.dot_general(ones8, lo, NT_DIMS, preferred_element_type=F32))
        a8 = _gelu(s8) * gate_scr[pl.ds(t, 1), :]
        ah, al = _split_bf16(a8)
        reps = picks // 8
        acol = (lax.dot_general(eye, jnp.concatenate([ah] * reps, axis=0), NT_DIMS, preferred_element_type=F32)
                + lax.dot_general(eye, jnp.concatenate([al] * reps, axis=0), NT_DIMS, preferred_element_type=F32))
        outs = [jnp.sum(acol * buf[chunks + c], axis=0, keepdims=True) for c in range(chunks)]
        f_scr[pl.ds(t, 1), :] = jnp.concatenate(outs, axis=1)

    def body(it, carry):
        @pl.when(it == n_it - 1)
        def _():
            nxt.wait()

        for s in range(nsets):
            t0 = (it * nsets + s) * grp
            for j in range(grp):
                wait_slot(s * grp + j)
            for j in range(grp):
                la = t0 + j + ahead
                cross = la >= tt
                issue(jnp.where(cross, 1 - par, par), jnp.where(cross, la - tt, la),
                      ((s + nsets - 1) % nsets) * grp + j)
                compute(s * grp + j, t0 + j)
        return carry

    lax.fori_loop(0, n_it, body, 0)

    @pl.when(tile == ntiles - 1)
    def _():
        for slot in range(ahead):
            wait_slot(slot)

    y_o[...] = x1_ref[...] + gt2_ref[0] * _rms(f_scr[...], gpost_ref[...])


def _peer_ffn(idx, h2, gate, x1, gt2, g_post, table, tt, tiles_per_batch, grp=8, nsets=4):
    n, d = h2.shape
    picks = gate.shape[0]
    chunks = d // LANES
    ntiles = n // tt
    assert table.shape[1:] == (2 * chunks, LANES) and tt % (grp * nsets) == 0 and picks % 8 == 0
    assert table.shape[0] % picks == 0
    idx_tiles = idx.reshape(picks, ntiles, tt).transpose(1, 2, 0)
    slot_view = table.reshape(-1, 2 * chunks, picks, LANES)
    kern = functools.partial(_peer_kernel, tt=tt, picks=picks, chunks=chunks, grp=grp, nsets=nsets,
                             ntiles=ntiles)
    row = lambda w: pl.BlockSpec((tt, w), lambda i: (i, 0))
    hbm = pl.BlockSpec(memory_space=pl.ANY)
    return pl.pallas_call(
        kern,
        grid=(ntiles,),
        in_specs=[hbm, hbm, hbm, row(d), pl.BlockSpec((picks, tt), lambda i: (0, i)), row(d),
                  _ada_spec(gt2, tt, tiles_per_batch), pl.BlockSpec(g_post.shape, lambda i: (0, 0))],
        out_specs=row(d),
        out_shape=jax.ShapeDtypeStruct((n, d), F32),
        scratch_shapes=[pltpu.SMEM((2, tt, picks), jnp.int32),
                        pltpu.VMEM((tt, picks), F32),
                        pltpu.VMEM((tt, d), F32),
                        pltpu.SemaphoreType.DMA((2,)),
                        pltpu.SemaphoreType.DMA((grp * nsets,))]
                       + [pltpu.VMEM((2 * chunks, picks, LANES), F32) for _ in range(grp * nsets)],
        compiler_params=_cparams(("arbitrary",)),
        name="peer_ffn",
    )(idx_tiles, slot_view, table, h2, gate, x1, gt2, g_post)


def _rope_table(pos, rdim, q_scale):
    half = rdim // 2
    inv = ROPE_BASE ** (-jnp.arange(half, dtype=F32) / half)
    ang = pos.astype(F32)[:, None] * inv[None, :]
    cos = jnp.concatenate([jnp.cos(ang), jnp.cos(ang)], axis=1)
    sin = jnp.concatenate([-jnp.sin(ang), jnp.sin(ang)], axis=1)
    t = pos.shape[0]
    z = lambda w: jnp.zeros((t, w), F32)
    cq = jnp.concatenate([jnp.ones((t, 64), F32), cos, z(64 - rdim)], axis=1) * q_scale
    sq = jnp.concatenate([z(64), sin, z(64 - rdim)], axis=1) * q_scale
    ck = jnp.concatenate([cos, z(64 - rdim), cos, z(64 - rdim)], axis=1)
    sk = jnp.concatenate([sin, z(64 - rdim), sin, z(64 - rdim)], axis=1)
    return jnp.concatenate([cq, sq, ck, sk], axis=1)


def _swap_halves(w):
    half = w.shape[-1] // 2
    return jnp.concatenate([w[..., half:], w[..., :half]], axis=-1)


def _prep_weights(w_in, g_q_lat, g_kv_lat, w_uq, w_uk, w_uv, dims):
    d = w_in.shape[0]
    sbw, qr, kvr, rdim = dims["sbw"], dims["qr"], dims["kvr"], dims["rdim"]
    heads, nope, vdim = dims["mla_heads"], dims["nope"], dims["vdim"]
    assert nope == 64 and rdim <= 64 and vdim == 64 and dims["sb_dim"] == 64
    o = 3 * sbw
    w_kr = w_in[:, o + qr + kvr:]
    zk = jnp.zeros((d, 64 - rdim), F32)
    kr_a = jnp.concatenate([w_kr, zk, w_kr, zk], axis=1)
    kr_b = jnp.concatenate([_swap_halves(w_kr), zk, _swap_halves(w_kr), zk], axis=1)
    w_all = jnp.concatenate([w_in[:, :o + qr + kvr], kr_a, kr_b], axis=1).astype(BF16)
    uq = w_uq.reshape(qr, heads, nope + rdim)
    zq = jnp.zeros((qr, heads, LANES - nope - rdim), F32)
    uq_a = jnp.concatenate([uq, zq], axis=2)
    uq_b = jnp.concatenate([jnp.zeros((qr, heads, nope), F32), _swap_halves(uq[..., nope:]), zq], axis=2)
    w_uq2 = jnp.concatenate([uq_a.reshape(qr, -1), uq_b.reshape(qr, -1)], axis=1).astype(BF16)
    uk = jnp.concatenate([w_uk, jnp.zeros((kvr, heads, LANES - nope), F32)], axis=2).reshape(kvr, -1)
    w_ukv = jnp.concatenate([uk, w_uv.reshape(kvr, -1)], axis=1).astype(BF16)
    place = (jnp.arange(rdim)[:, None] + 64 == jnp.arange(LANES)[None, :]).astype(BF16)
    w_ukt_pad = jnp.pad(jnp.transpose(w_uk, (1, 2, 0)), ((0, 0), (0, LANES - nope), (0, 0))).astype(BF16)
    head_cols = (jnp.arange(heads)[:, None, None] == jnp.arange(heads)[None, :, None]) * jnp.ones((1, 1, vdim), F32)
    w_uv_pad = (w_uv[None] * head_cols[:, None]).reshape(heads, kvr, heads * vdim).astype(BF16)
    return dict(w_all=w_all, w_uq2=w_uq2, w_ukv=w_ukv, place=place, w_ukt_pad=w_ukt_pad, w_uv_pad=w_uv_pad,
                g_q=g_q_lat.reshape(1, -1), g_kv=g_kv_lat.reshape(1, -1))


def _layer(x, ada, pos0, past, lw, dims, tq_sb, tq_mla, tk_sb, tk_mla, tk_dec, tm_in, tm_out, tt):
    b, t, d = x.shape
    n = b * t
    x2 = x.reshape(n, d)
    per_token = t < tm_in
    if per_token:
        sh1, sc1, gt1, sh2, sc2, gt2 = [jnp.repeat(a, t, axis=0)[None] for a in ada]
        tiles_in = tiles_out = tiles_tt = 1
    else:
        sh1, sc1, gt1, sh2, sc2, gt2 = [a[:, None, :] for a in ada]
        tiles_in, tiles_out, tiles_tt = t // tm_in, t // tm_out, t // tt
    tm_in, tm_out, tt = min(tm_in, n), min(tm_out, n), min(tt, n)
    heads = dims["mla_heads"]
    q_scale = (dims["nope"] + dims["rdim"]) ** -0.5
    rope_tab = _rope_table(pos0 + jnp.arange(t), dims["rdim"], q_scale)
    rope_tab = jnp.tile(rope_tab, (b, 1)) if per_token else rope_tab
    (sbq, sbk, sbv, sbkb, sbvb, ckvn, kr, qcat, kcat, vmla) = _in_proj(
        x2, sc1, sh1, lw["g_mix_pre"], rope_tab, lw, dims, tm_in, tiles_in)
    sbw, gw, vw = dims["sbw"], heads * LANES, vmla.shape[1]
    r3 = lambda a: a.reshape(b, t, a.shape[-1])
    if past is None:
        sbo = _sb_attn(r3(sbq), r3(sbkb), r3(sbvb), tq_sb, tk_sb, 0)
        mlao = _mla_attn(r3(qcat), r3(kcat), r3(vmla), tq_mla, tk_mla, 0, t)
    else:
        c_k, c_v, c_ckv, c_kr = past
        p_len = c_k.shape[1]
        pos_minor = lambda a: jnp.moveaxis(a, 1, -1)
        sbo, mlao = _dec_attn(r3(sbq), r3(sbkb), r3(sbvb), pos_minor(c_k), pos_minor(c_v), r3(qcat), r3(ckvn),
                              r3(kr), c_ckv, pos_minor(c_kr), lw, tk_dec)
    x1, h2, idx, gate = _out_peer(x2, sbo.reshape(n, sbw), mlao.reshape(n, vw), gt1, sc2, sh2, lw, dims,
                                  tm_out, tiles_out)
    picks = dims["peer_heads"] * PEER_TOPK
    y = _peer_ffn(idx.reshape(picks, n), h2, gate.reshape(picks, n), x1, gt2, lw["g_post_ffn"], lw["table"],
                  tt, tiles_tt)
    hd = (dims["sb_heads"], dims["sb_dim"])
    if per_token:
        new = (sbk.reshape(b, t, *hd), sbv.reshape(b, t, *hd), ckvn.reshape(b, t, -1), kr.reshape(b, t, -1))
    else:
        new = (jnp.moveaxis(sbk.reshape(b, *hd, t), -1, 1), jnp.moveaxis(sbv.reshape(b, *hd, t), -1, 1),
               ckvn.reshape(b, t, -1), jnp.moveaxis(kr, -1, 1))
    return y.reshape(b, t, d), new


def kernel(x_prompt, x_sample, c_prompt, c_sample, cache_sb_k, cache_sb_v, cache_mla_ckv, cache_mla_krope,
           w_ada, b_ada, g_mix_pre, g_mix_post, g_ffn_pre, g_ffn_post, w_in, g_q_lat, g_kv_lat, w_uq, w_uk,
           w_uv, g_sb_out, g_mla_out, w_out, w_peer_q, sub_keys_1, sub_keys_2, u_experts, v_experts):
    depth = w_ada.shape[0]
    d = x_prompt.shape[-1]
    sb_heads, sb_dim = cache_sb_k.shape[-2:]
    kvr, mla_heads, nope = w_uk.shape[1:]
    dims = dict(sb_heads=sb_heads, sb_dim=sb_dim, sbw=sb_heads * sb_dim, qr=w_uq.shape[1], kvr=kvr,
                rdim=cache_mla_krope.shape[-1], mla_heads=mla_heads, nope=nope, vdim=w_uv.shape[-1],
                n_keys=sub_keys_1.shape[1], peer_heads=w_peer_q.shape[2] // (2 * sub_keys_1.shape[2]))
    bp, bs = c_prompt.shape[0], c_sample.shape[0]
    pad = (-(bp + bs)) % 8
    y_p, y_s = x_prompt, x_sample
    outs_p, outs_s = [], []
    for l in range(depth):
        c_all = jnp.concatenate([c_prompt, c_sample, jnp.zeros((pad, d), F32)], axis=0)
        ada = _ada(c_all, w_ada[l], b_ada[l])
        ada_p = jnp.split(ada[:bp], 6, axis=1)
        ada_s = jnp.split(ada[bp:bp + bs], 6, axis=1)
        lw = _prep_weights(w_in[l], g_q_lat[l], g_kv_lat[l], w_uq[l], w_uk[l], w_uv[l], dims)
        e = u_experts.shape[1]
        lw.update(
            g_mix_pre=g_mix_pre[l].reshape(1, d), g_post=g_mix_post[l].reshape(1, d),
            g_ffn=g_ffn_pre[l].reshape(1, d), g_post_ffn=g_ffn_post[l].reshape(1, d),
            g_sb=g_sb_out[l].reshape(1, -1), g_mla=g_mla_out[l].reshape(1, -1),
            w_out=w_out[l].astype(BF16), w_pq=w_peer_q[l].astype(BF16),
            k1=sub_keys_1[l].astype(BF16), k2=sub_keys_2[l].astype(BF16),
            table=jnp.concatenate([u_experts[l].reshape(e, d // LANES, LANES),
                                   v_experts[l].reshape(e, d // LANES, LANES)], axis=1))
        tiles = dict(tq_sb=256, tq_mla=1024, tk_sb=256, tk_mla=1024, tk_dec=512, tm_in=512, tm_out=256, tt=256)
        y_p, new_p = _layer(y_p, ada_p, 0, None, lw, dims, **tiles)
        past = (cache_sb_k[l], cache_sb_v[l], cache_mla_ckv[l], cache_mla_krope[l])
        y_s, new_s = _layer(y_s, ada_s, cache_sb_k.shape[2], past, lw, dims, **tiles)
        outs_p.append(new_p)
        outs_s.append(new_s)
    stack = lambda outs, k: jnp.stack([o[k] for o in outs])
    return (y_p, y_s, stack(outs_p, 0), stack(outs_p, 1), stack(outs_p, 2), stack(outs_p, 3),
            stack(outs_s, 0), stack(outs_s, 1), stack(outs_s, 2), stack(outs_s, 3))
```
